```python
import jax, jax.numpy as jnp
from jax import lax
import numpy as np

D_MODEL = 1024
BATCH = 8
SEQ = 4096
DEPTH = 2
DEC_BATCH = 32
DEC_SEQ = 64
PAST_LEN = 1024

CHUNK = 64
HEAD_DIM = 64
H_A = 8
Q_LORA = 768
KV_LORA = 256
NOPE_DIM = 64
ROPE_DIM = 32
V_DIM = 64
ROPE_THETA = 10000.0
MLA_SCALE = (NOPE_DIM + ROPE_DIM) ** -0.5
H_B = 8
PREV_CHUNKS = 8
BAND_PAST = PREV_CHUNKS * CHUNK
REL_MAX = 256
N_REL = REL_MAX + CHUNK
BAND_SCALE = HEAD_DIM ** -0.5
H_C = 16
SB_SCALE = HEAD_DIM ** -0.5
Q_BLOCK = 128
IN_AB = Q_LORA + KV_LORA + ROPE_DIM + 3 * H_B * HEAD_DIM
MIX_AB = H_A * V_DIM + H_B * HEAD_DIM
IN_C = 3 * H_C * HEAD_DIM
MIX_C = H_C * HEAD_DIM
D_FF = 4 * D_MODEL
N_EVEN = (DEPTH + 1) // 2
N_ODD = DEPTH // 2
ALPHA = (2.0 * DEPTH) ** 0.25
BETA = (8.0 * DEPTH) ** -0.25
NEG_INF = -1e30

kernel_name = "hybrid_mla_band_stickbreak_stream_step"


def _layer_norm(x, g, b, eps=1e-5):
    xf = x.astype(jnp.float32)
    mu = jnp.mean(xf, -1, keepdims=True)
    var = jnp.mean(jnp.square(xf - mu), -1, keepdims=True)
    return ((xf - mu) * lax.rsqrt(var + eps) * g.astype(jnp.float32) + b.astype(jnp.float32)).astype(x.dtype)


def _rms_norm(x, g, eps=1e-6):
    xf = x.astype(jnp.float32)
    return (xf * lax.rsqrt(jnp.mean(jnp.square(xf), -1, keepdims=True) + eps) * g.astype(jnp.float32)).astype(x.dtype)


def _rope(x, pos):
    half = ROPE_DIM // 2
    inv = jnp.power(ROPE_THETA, -jnp.arange(half, dtype=jnp.float32) / half)
    ang = pos.astype(jnp.float32)[:, None] * inv[None, :]
    ang = ang.reshape((pos.shape[0],) + (1,) * (x.ndim - 3) + (half,))
    cos, sin = jnp.cos(ang), jnp.sin(ang)
    xf = x.astype(jnp.float32)
    x1, x2 = xf[..., :half], xf[..., half:]
    return jnp.concatenate([x1 * cos - x2 * sin, x2 * cos + x1 * sin], -1).astype(x.dtype)


def _chunk_mask(q_pos, k_pos):
    return (k_pos[None, :] // CHUNK) <= (q_pos[:, None] // CHUNK)


def _softmax_attend(q, k, v, mask, scale):
    s = jnp.einsum('bqhd,bkhd->bhqk', q, k).astype(jnp.float32) * scale
    p = jax.nn.softmax(jnp.where(mask, s, NEG_INF), axis=-1)
    return jnp.einsum('bhqk,bkhd->bqhd', p.astype(v.dtype), v)


def _stick_breaking(q, k, v, q_pos, k_pos):
    z = jnp.einsum('bqhd,bkhd->bhqk', q, k).astype(jnp.float32) * SB_SCALE
    mask = k_pos[None, :] < q_pos[:, None]
    log_1m = jnp.where(mask, jax.nn.log_sigmoid(-z), 0.0)
    tail = lax.cumsum(log_1m, axis=z.ndim - 1, reverse=True) - log_1m
    w = jnp.where(mask, jnp.exp(jax.nn.log_sigmoid(z) + tail), 0.0)
    return jnp.einsum('bhqk,bkhd->bqhd', w.astype(v.dtype), v)


def _blockwise(fn, q, q_pos):
    B, S = q.shape[:2]
    nb = S // Q_BLOCK
    qb = jnp.moveaxis(q.reshape((B, nb, Q_BLOCK) + q.shape[2:]), 1, 0)
    pb = q_pos.reshape(nb, Q_BLOCK)
    out = lax.map(lambda a: fn(a[0], a[1]), (qb, pb))
    return jnp.moveaxis(out, 0, 1).reshape((B, S) + out.shape[3:])


def _rel_index(dist):
    return jnp.clip(dist, -(CHUNK - 1), REL_MAX) + (CHUNK - 1)


def _mla_keys(ckv, kr, w_ukv):
    B, T = ckv.shape[:2]
    kv = (ckv @ w_ukv).reshape(B, T, H_A, NOPE_DIM + V_DIM)
    k = jnp.concatenate([kv[..., :NOPE_DIM], jnp.broadcast_to(kr[:, :, None, :], (B, T, H_A, ROPE_DIM))], -1)
    return k, kv[..., NOPE_DIM:]


def _even_project(x, pos, w_in, g_q, w_uq, g_kv):
    B, S, _ = x.shape
    h = x @ w_in
    o1 = Q_LORA
    o2 = o1 + KV_LORA
    o3 = o2 + ROPE_DIM
    q_a = (_rms_norm(h[..., :o1], g_q) @ w_uq).reshape(B, S, H_A, NOPE_DIM + ROPE_DIM)
    q_a = jnp.concatenate([q_a[..., :NOPE_DIM], _rope(q_a[..., NOPE_DIM:], pos)], -1)
    ckv = _rms_norm(h[..., o1:o2], g_kv)
    kr = _rope(h[..., o2:o3], pos)
    hb = h[..., o3:].reshape(B, S, 3, H_B, HEAD_DIM)
    return q_a, ckv, kr, hb[:, :, 0], hb[:, :, 1], hb[:, :, 2]


def _band_prompt(q, k, v, table):
    B, S, H, d = q.shape
    n_c = S // CHUNK
    n_band = PREV_CHUNKS + 1
    qc = q.reshape(B, n_c, CHUNK, H, d)
    pad = ((0, 0), (PREV_CHUNKS, 0), (0, 0), (0, 0), (0, 0))
    kc = jnp.pad(k.reshape(B, n_c, CHUNK, H, d), pad)
    vc = jnp.pad(v.reshape(B, n_c, CHUNK, H, d), pad)
    s = jnp.stack([jnp.einsum('bnihd,bnjhd->bhnij', qc, kc[:, PREV_CHUNKS - o:PREV_CHUNKS - o + n_c])
                   for o in range(n_band)], axis=-2).astype(jnp.float32) * BAND_SCALE
    off = jnp.arange(n_band)
    i = jnp.arange(CHUNK)
    dist = off[None, :, None] * CHUNK + i[:, None, None] - i[None, None, :]
    bias = table[:, _rel_index(dist)].astype(jnp.float32)
    valid = jnp.arange(n_c)[:, None] >= off[None, :]
    s = jnp.where(valid[None, None, :, None, :, None], s + bias[None, :, None], NEG_INF)
    p = jax.nn.softmax(s.reshape(B, H, n_c, CHUNK, n_band * CHUNK), axis=-1).reshape(s.shape).astype(v.dtype)
    out = sum(jnp.einsum('bhnij,bnjhd->bnihd', p[..., o, :], vc[:, PREV_CHUNKS - o:PREV_CHUNKS - o + n_c])
              for o in range(n_band))
    return out.reshape(B, S, H, d)


def _band_sample(q, k_all, v_all, n_past, table):
    T = q.shape[1]
    k_rel = jnp.arange(k_all.shape[1]) - n_past
    dist = jnp.arange(T)[:, None] - k_rel[None, :]
    bias = table[:, _rel_index(dist)].astype(jnp.float32)
    s = jnp.einsum('bqhd,bkhd->bhqk', q, k_all).astype(jnp.float32) * BAND_SCALE + bias[None]
    p = jax.nn.softmax(s, axis=-1)
    return jnp.einsum('bhqk,bkhd->bqhd', p.astype(v_all.dtype), v_all)


def _even_prompt(x, pos, w_in, g_q, w_uq, g_kv, w_ukv, table, w_out):
    B, S, _ = x.shape
    q_a, ckv, kr, q_b, k_b, v_b = _even_project(x, pos, w_in, g_q, w_uq, g_kv)
    k_a, v_a = _mla_keys(ckv, kr, w_ukv)
    o_a = _blockwise(lambda qb, pb: _softmax_attend(qb, k_a, v_a, _chunk_mask(pb, pos), MLA_SCALE), q_a, pos)
    o_b = _band_prompt(q_b, k_b, v_b, table)
    y = jnp.concatenate([o_a.reshape(B, S, H_A * V_DIM), o_b.reshape(B, S, H_B * HEAD_DIM)], -1) @ w_out
    rows = min(BAND_PAST, S)
    return y, ckv, kr, k_b[:, S - rows:], v_b[:, S - rows:]


def _even_sample(x, pos, c_ckv, c_kr, c_bk, c_bv, w_in, g_q, w_uq, g_kv, w_ukv, table, w_out):
    B, T, _ = x.shape
    q_a, ckv, kr, q_b, k_b, v_b = _even_project(x, pos, w_in, g_q, w_uq, g_kv)
    ckv_all = jnp.concatenate([c_ckv, ckv], 1)
    kr_all = jnp.concatenate([c_kr, kr], 1)
    k_a, v_a = _mla_keys(ckv_all, kr_all, w_ukv)
    o_a = _softmax_attend(q_a, k_a, v_a, _chunk_mask(pos, jnp.arange(ckv_all.shape[1])), MLA_SCALE)
    kb_all = jnp.concatenate([c_bk, k_b], 1)
    vb_all = jnp.concatenate([c_bv, v_b], 1)
    o_b = _band_sample(q_b, kb_all, vb_all, c_bk.shape[1], table)
    y = jnp.concatenate([o_a.reshape(B, T, H_A * V_DIM), o_b.reshape(B, T, H_B * HEAD_DIM)], -1) @ w_out
    rows = min(BAND_PAST, kb_all.shape[1])
    return y, ckv, kr, kb_all[:, kb_all.shape[1] - rows:], vb_all[:, vb_all.shape[1] - rows:]


def _odd_project(x, w_in):
    B, S, _ = x.shape
    h = (x @ w_in).reshape(B, S, 3, H_C, HEAD_DIM)
    return h[:, :, 0], h[:, :, 1], h[:, :, 2]


def _odd_prompt(x, pos, w_in, w_out):
    B, S, _ = x.shape
    q, k, v = _odd_project(x, w_in)
    o = _blockwise(lambda qb, pb: _stick_breaking(qb, k, v, pb, pos), q, pos)
    return o.reshape(B, S, MIX_C) @ w_out, k, v


def _odd_sample(x, pos, c_k, c_v, w_in, w_out):
    B, T, _ = x.shape
    q, k, v = _odd_project(x, w_in)
    k_all = jnp.concatenate([c_k, k], 1)
    v_all = jnp.concatenate([c_v, v], 1)
    o = _stick_breaking(q, k_all, v_all, pos, jnp.arange(k_all.shape[1]))
    return o.reshape(B, T, MIX_C) @ w_out, k, v


def _post_block(x, mix, g1, b1, g2, b2, w_up, w_down):
    x = _layer_norm(ALPHA * x + mix, g1, b1)
    ff = jnp.square(jax.nn.relu(x @ w_up)) @ w_down
    return _layer_norm(ALPHA * x + ff, g2, b2)


def setup_inputs(seed: int = 0) -> dict:
    key = jax.random.key(seed)
    ks = jax.random.split(key, 24)

    def nrm(k, shape, scale):
        return jax.random.normal(k, shape, jnp.float32) * scale

    band_rows = min(BAND_PAST, PAST_LEN)
    return {
        "x_prompt": nrm(ks[0], (BATCH, SEQ, D_MODEL), 1.0),
        "x_sample": nrm(ks[1], (DEC_BATCH, DEC_SEQ, D_MODEL), 1.0),
        "cache_mla_ckv": nrm(ks[2], (N_EVEN, DEC_BATCH, PAST_LEN, KV_LORA), 1.0),
        "cache_mla_krope": nrm(ks[3], (N_EVEN, DEC_BATCH, PAST_LEN, ROPE_DIM), 1.0),
        "cache_band_k": nrm(ks[4], (N_EVEN, DEC_BATCH, band_rows, H_B, HEAD_DIM), 1.0),
        "cache_band_v": nrm(ks[5], (N_EVEN, DEC_BATCH, band_rows, H_B, HEAD_DIM), 1.0),
        "cache_sb_k": nrm(ks[6], (N_ODD, DEC_BATCH, PAST_LEN, H_C, HEAD_DIM), 1.0),
        "cache_sb_v": nrm(ks[7], (N_ODD, DEC_BATCH, PAST_LEN, H_C, HEAD_DIM), 1.0),
        "w_in_ab": nrm(ks[8], (N_EVEN, D_MODEL, IN_AB), D_MODEL ** -0.5),
        "g_q_lat": 1.0 + nrm(ks[9], (N_EVEN, Q_LORA), 0.02),
        "w_uq": nrm(ks[10], (N_EVEN, Q_LORA, H_A * (NOPE_DIM + ROPE_DIM)), Q_LORA ** -0.5),
        "g_kv_lat": 1.0 + nrm(ks[11], (N_EVEN, KV_LORA), 0.02),
        "w_ukv": nrm(ks[12], (N_EVEN, KV_LORA, H_A * (NOPE_DIM + V_DIM)), KV_LORA ** -0.5),
        "rel_bias": nrm(ks[13], (N_EVEN, H_B, N_REL), 0.1),
        "w_out_ab": nrm(ks[14], (N_EVEN, MIX_AB, D_MODEL), BETA * MIX_AB ** -0.5),
        "w_in_c": nrm(ks[15], (N_ODD, D_MODEL, IN_C), D_MODEL ** -0.5),
        "w_out_c": nrm(ks[16], (N_ODD, MIX_C, D_MODEL), BETA * MIX_C ** -0.5),
        "ln_mix_g": 1.0 + nrm(ks[17], (DEPTH, D_MODEL), 0.02),
        "ln_mix_b": nrm(ks[18], (DEPTH, D_MODEL), 0.02),
        "ln_ffn_g": 1.0 + nrm(ks[19], (DEPTH, D_MODEL), 0.02),
        "ln_ffn_b": nrm(ks[20], (DEPTH, D_MODEL), 0.02),
        "w_ff_up": nrm(ks[21], (DEPTH, D_MODEL, D_FF), D_MODEL ** -0.5),
        "w_ff_down": nrm(ks[22], (DEPTH, D_FF, D_MODEL), BETA * D_FF ** -0.5),
    }


def reference(x_prompt, x_sample, cache_mla_ckv, cache_mla_krope, cache_band_k, cache_band_v,
              cache_sb_k, cache_sb_v, w_in_ab, g_q_lat, w_uq, g_kv_lat, w_ukv, rel_bias, w_out_ab,
              w_in_c, w_out_c, ln_mix_g, ln_mix_b, ln_ffn_g, ln_ffn_b, w_ff_up, w_ff_down):
    pos_p = jnp.arange(x_prompt.shape[1], dtype=jnp.int32)
    n_past = cache_mla_ckv.shape[2]
    pos_s = n_past + jnp.arange(x_sample.shape[1], dtype=jnp.int32)
    xp, xs = x_prompt, x_sample
    p_ckv, p_kr, p_bk, p_bv, p_sk, p_sv = [], [], [], [], [], []
    s_ckv, s_kr, s_bk, s_bv, s_sk, s_sv = [], [], [], [], [], []
    for l in range(DEPTH):
        i = l // 2
        if l % 2 == 0:
            mp, a1, a2, a3, a4 = _even_prompt(xp, pos_p, w_in_ab[i], g_q_lat[i], w_uq[i], g_kv_lat[i],
                                              w_ukv[i], rel_bias[i], w_out_ab[i])
            ms, b1, b2, b3, b4 = _even_sample(xs, pos_s, cache_mla_ckv[i], cache_mla_krope[i],
                                              cache_band_k[i], cache_band_v[i], w_in_ab[i], g_q_lat[i],
                                              w_uq[i], g_kv_lat[i], w_ukv[i], rel_bias[i], w_out_ab[i])
            p_ckv.append(a1); p_kr.append(a2); p_bk.append(a3); p_bv.append(a4)
            s_ckv.append(b1); s_kr.append(b2); s_bk.append(b3); s_bv.append(b4)
        else:
            mp, a1, a2 = _odd_prompt(xp, pos_p, w_in_c[i], w_out_c[i])
            ms, b1, b2 = _odd_sample(xs, pos_s, cache_sb_k[i], cache_sb_v[i], w_in_c[i], w_out_c[i])
            p_sk.append(a1); p_sv.append(a2)
            s_sk.append(b1); s_sv.append(b2)
        xp = _post_block(xp, mp, ln_mix_g[l], ln_mix_b[l], ln_ffn_g[l], ln_ffn_b[l], w_ff_up[l], w_ff_down[l])
        xs = _post_block(xs, ms, ln_mix_g[l], ln_mix_b[l], ln_ffn_g[l], ln_ffn_b[l], w_ff_up[l], w_ff_down[l])
    return (xp, xs,
            jnp.stack(p_ckv), jnp.stack(p_kr), jnp.stack(p_bk), jnp.stack(p_bv), jnp.stack(p_sk), jnp.stack(p_sv),
            jnp.stack(s_ckv), jnp.stack(s_kr), jnp.stack(s_bk), jnp.stack(s_bv), jnp.stack(s_sk), jnp.stack(s_sv))
```

```python
import functools

import jax
import jax.numpy as jnp
from jax import lax
from jax.experimental import pallas as pl
from jax.experimental.pallas import tpu as pltpu

F32 = jnp.float32
BF16 = jnp.bfloat16

CHUNK = 64
CHUNK_SHIFT = 6
HEAD_DIM = 64
H_A = 8
Q_LORA = 768
KV_LORA = 256
NOPE_DIM = 64
ROPE_DIM = 32
V_DIM = 64
ROPE_THETA = 10000.0
MLA_SCALE = (NOPE_DIM + ROPE_DIM) ** -0.5
H_B = 8
PREV_CHUNKS = 8
BAND_PAST = PREV_CHUNKS * CHUNK
REL_MAX = 256
N_REL = REL_MAX + CHUNK
H_C = 16
QK_SCALE = HEAD_DIM ** -0.5
NEG_INF = -1e30

LANES = 128
V7X_VMEM_LIMIT_BYTES = 56 * 1024 * 1024

MLA_HEAD_LANES = LANES
ROPE_LANE0 = NOPE_DIM
ROPE_HALF = ROPE_DIM // 2

ROW_TILE = 512


def _params(n_axes, vmem_bytes=None):
    return pltpu.CompilerParams(
        dimension_semantics=("parallel",) * n_axes,
        vmem_limit_bytes=vmem_bytes,
    )


def _resident(shape):
    nd = len(shape)
    return pl.BlockSpec(shape, lambda *_: (0,) * nd, pipeline_mode=pl.Buffered(1))


def _rows(tm, width):
    return pl.BlockSpec((tm, width), lambda i: (i, 0))


def _dot(a, b):
    return jnp.dot(a, b, preferred_element_type=F32)


def _dot_nt(a, b):
    return lax.dot_general(a, b, (((1,), (1,)), ((), ())), preferred_element_type=F32)


def _rms(h, g, eps=1e-6):
    return h * lax.rsqrt(jnp.mean(h * h, axis=-1, keepdims=True) + eps) * g


def _layer_norm(r, g, b, eps=1e-5):
    mu = jnp.mean(r, axis=-1, keepdims=True)
    d = r - mu
    var = jnp.mean(d * d, axis=-1, keepdims=True)
    return d * lax.rsqrt(var + eps) * g + b


N_BAND = 3 * H_B * HEAD_DIM
W1_Q0, W1_Q1 = 0, Q_LORA
W1_C0, W1_C1 = W1_Q1, W1_Q1 + KV_LORA
W1_R0, W1_R1 = W1_C1, W1_C1 + LANES
W1_B0, W1_B1 = W1_R1, W1_R1 + N_BAND
HB = H_B * HEAD_DIM


def _even_proj_kernel(x_ref, w1_ref, gq_ref, gkv_ref, wuq_ref, c_ref, s1_ref, s2_ref,
                      qpad_ref, ckv_ref, krw_ref, qb_ref, kb32_ref, vb32_ref, kb16_ref, vb16_ref):
    xb = x_ref[...].astype(BF16)
    c = c_ref[...]
    s1 = s1_ref[...]
    s2 = s2_ref[...]

    def rope(v):
        return (v * c + pltpu.roll(v, LANES - ROPE_HALF, 1) * s1
                + pltpu.roll(v, ROPE_HALF, 1) * s2)

    hq = _dot(xb, w1_ref[:, W1_Q0:W1_Q1])
    qn = _rms(hq, gq_ref[...]).astype(BF16)
    qa = _dot(qn, wuq_ref[...])
    for h in range(H_A):
        sl = slice(h * MLA_HEAD_LANES, (h + 1) * MLA_HEAD_LANES)
        qpad_ref[:, sl] = rope(qa[:, sl]).astype(BF16)

    hc = _dot(xb, w1_ref[:, W1_C0:W1_C1])
    ckv_ref[...] = _rms(hc, gkv_ref[...])

    hk = _dot(xb, w1_ref[:, W1_R0:W1_R1])
    krw_ref[...] = rope(hk)

    hb = _dot(xb, w1_ref[:, W1_B0:W1_B1])
    qb_ref[...] = (hb[:, 0:HB] * QK_SCALE).astype(BF16)
    kb = hb[:, HB:2 * HB]
    vb = hb[:, 2 * HB:3 * HB]
    kb32_ref[...] = kb
    vb32_ref[...] = vb
    kb16_ref[...] = kb.astype(BF16)
    vb16_ref[...] = vb.astype(BF16)


def _even_proj(x, w1, gq, gkv, wuq, rope_tabs):
    t, d = x.shape
    tm = min(ROW_TILE, t)
    c, s1, s2 = rope_tabs
    qw = H_A * MLA_HEAD_LANES
    out_shape = (
        jax.ShapeDtypeStruct((t, qw), BF16),
        jax.ShapeDtypeStruct((t, KV_LORA), F32),
        jax.ShapeDtypeStruct((t, LANES), F32),
        jax.ShapeDtypeStruct((t, HB), BF16),
        jax.ShapeDtypeStruct((t, HB), F32),
        jax.ShapeDtypeStruct((t, HB), F32),
        jax.ShapeDtypeStruct((t, HB), BF16),
        jax.ShapeDtypeStruct((t, HB), BF16),
    )
    return pl.pallas_call(
        _even_proj_kernel,
        grid=(t // tm,),
        in_specs=[_rows(tm, d), _resident(w1.shape), _resident(gq.shape), _resident(gkv.shape),
                  _resident(wuq.shape), _rows(tm, LANES), _rows(tm, LANES), _rows(tm, LANES)],
        out_specs=(_rows(tm, qw), _rows(tm, KV_LORA), _rows(tm, LANES), _rows(tm, HB),
                   _rows(tm, HB), _rows(tm, HB), _rows(tm, HB), _rows(tm, HB)),
        out_shape=out_shape,
        compiler_params=_params(1, V7X_VMEM_LIMIT_BYTES),
        name="even_in_proj",
    )(x, w1, gq, gkv, wuq, c, s1, s2)


def _kv_up_kernel(ckv_ref, krw_ref, wk_ref, wv_ref, kpad_ref, v_ref):
    cb = ckv_ref[...].astype(BF16)
    krw = krw_ref[...]
    k = _dot(cb, wk_ref[...])
    for h in range(H_A):
        sl = slice(h * MLA_HEAD_LANES, (h + 1) * MLA_HEAD_LANES)
        kpad_ref[:, sl] = (k[:, sl] + krw).astype(BF16)
    v_ref[...] = _dot(cb, wv_ref[...]).astype(BF16)


def _kv_up(ckv, krw, wk, wv):
    t = ckv.shape[0]
    tm = min(ROW_TILE, t)
    kw = H_A * MLA_HEAD_LANES
    vw = H_A * V_DIM
    return pl.pallas_call(
        _kv_up_kernel,
        grid=(t // tm,),
        in_specs=[_rows(tm, KV_LORA), _rows(tm, LANES), _resident(wk.shape), _resident(wv.shape)],
        out_specs=(_rows(tm, kw), _rows(tm, vw)),
        out_shape=(jax.ShapeDtypeStruct((t, kw), BF16), jax.ShapeDtypeStruct((t, vw), BF16)),
        compiler_params=_params(1),
        name="mla_kv_up",
    )(ckv, krw, wk, wv)


def _pair_select(lo, hi):
    lane = lax.broadcasted_iota(jnp.int32, lo.shape, 1)
    return jnp.where(lane < HEAD_DIM, lo, hi)


def _mla_kernel(q_ref, k_ref, v_ref, o_ref, *, tq, tk, causal, kv_len):
    i = pl.program_id(2)
    sk = k_ref.shape[0]
    n_kb = ((i + 1) * tq + tk - 1) // tk if causal else sk // tk
    row = lax.broadcasted_iota(jnp.int32, (tq, tk), 0)
    col = lax.broadcasted_iota(jnp.int32, (tq, tk), 1)
    q_chunk = (row + i * tq) >> CHUNK_SHIFT
    outs = []
    for hh in range(2):
        hsl = slice(hh * MLA_HEAD_LANES, (hh + 1) * MLA_HEAD_LANES)
        q = q_ref[:, hsl]

        def body(j, carry, q=q, hsl=hsl):
            m, l, acc = carry
            start = pl.multiple_of(j * tk, tk)
            s = _dot_nt(q, k_ref[pl.ds(start, tk), hsl]) * MLA_SCALE
            kidx = col + j * tk
            if causal:
                s = jnp.where((kidx >> CHUNK_SHIFT) <= q_chunk, s, NEG_INF)
            elif kv_len < sk:
                s = jnp.where(kidx < kv_len, s, NEG_INF)
            m_new = jnp.maximum(m, jnp.max(s, axis=-1, keepdims=True))
            alpha = jnp.exp(m - m_new)
            p = jnp.exp(s - m_new)
            l = alpha * l + jnp.sum(p, axis=-1, keepdims=True)
            acc = alpha * acc + _dot(p.astype(BF16), v_ref[pl.ds(start, tk), :])
            return m_new, l, acc

        init = (jnp.full((tq, 1), NEG_INF, F32), jnp.zeros((tq, 1), F32),
                jnp.zeros((tq, 2 * V_DIM), F32))
        _, l, acc = lax.fori_loop(0, n_kb, body, init)
        outs.append(acc / l)
    o_ref[...] = _pair_select(outs[0], outs[1]).astype(o_ref.dtype)


def _mla_attn(q, k, v, *, tq, tk, causal, kv_len):
    nb, sq, _ = q.shape
    sk = k.shape[1]
    pairs = H_A // 2
    qk_w = 2 * MLA_HEAD_LANES
    v_w = 2 * V_DIM
    kern = functools.partial(_mla_kernel, tq=tq, tk=tk, causal=causal, kv_len=kv_len)
    return pl.pallas_call(
        kern,
        grid=(nb, pairs, sq // tq),
        in_specs=[pl.BlockSpec((None, tq, qk_w), lambda b, p, i: (b, i, p)),
                  pl.BlockSpec((None, sk, qk_w), lambda b, p, i: (b, 0, p)),
                  pl.BlockSpec((None, sk, v_w), lambda b, p, i: (b, 0, p))],
        out_specs=pl.BlockSpec((None, tq, v_w), lambda b, p, i: (b, i, p)),
        out_shape=jax.ShapeDtypeStruct((nb, sq, H_A * V_DIM), BF16),
        compiler_params=_params(3),
        name="mla_attn",
    )(q, k, v)


def _half_masks(x):
    lane = lax.broadcasted_iota(jnp.int32, x.shape, 1)
    zero = jnp.zeros_like(x)
    return jnp.where(lane < HEAD_DIM, x, zero), jnp.where(lane < HEAD_DIM, zero, x)


def _band_kernel(q_ref, k_ref, v_ref, bias_ref, o_ref, *, tq, tk, nkb):
    i = pl.program_id(2)
    outs = []
    for hh, q in enumerate(_half_masks(q_ref[...])):
        scores = []
        starts = []
        for d in range(nkb):
            kbi = i - (nkb - 1) + d
            start = pl.multiple_of(jnp.maximum(kbi, 0) * tk, tk)
            s = _dot_nt(q, k_ref[pl.ds(start, tk), :]) + bias_ref[hh, d]
            scores.append(jnp.where(kbi >= 0, s, NEG_INF))
            starts.append(start)
        m = scores[0].max(axis=-1, keepdims=True)
        for s in scores[1:]:
            m = jnp.maximum(m, s.max(axis=-1, keepdims=True))
        l = jnp.zeros((tq, 1), F32)
        acc = jnp.zeros((tq, 2 * HEAD_DIM), F32)
        for s, start in zip(scores, starts):
            p = jnp.exp(s - m)
            l = l + jnp.sum(p, axis=-1, keepdims=True)
            acc = acc + _dot(p.astype(BF16), v_ref[pl.ds(start, tk), :])
        outs.append(acc / l)
    o_ref[...] = _pair_select(outs[0], outs[1]).astype(o_ref.dtype)


def _band_attn(q, k, v, bias, *, tq, tk, nkb):
    nb, sq, w = q.shape
    sk = k.shape[1]
    pairs = H_B // 2
    pw = 2 * HEAD_DIM
    assert bias.shape == (H_B, nkb, tq, tk)
    assert tq == tk or (nkb == 1 and sq == tq and sk == tk)
    kern = functools.partial(_band_kernel, tq=tq, tk=tk, nkb=nkb)
    return pl.pallas_call(
        kern,
        grid=(nb, pairs, sq // tq),
        in_specs=[pl.BlockSpec((None, tq, pw), lambda b, p, i: (b, i, p)),
                  pl.BlockSpec((None, sk, pw), lambda b, p, i: (b, 0, p)),
                  pl.BlockSpec((None, sk, pw), lambda b, p, i: (b, 0, p)),
                  pl.BlockSpec((2, nkb, tq, tk), lambda b, p, i: (p, 0, 0, 0))],
        out_specs=pl.BlockSpec((None, tq, pw), lambda b, p, i: (b, i, p)),
        out_shape=jax.ShapeDtypeStruct((nb, sq, w), BF16),
        compiler_params=_params(3),
        name="band_attn",
    )(q, k, v, bias)


def _band_bias_kernel(tab_ref, o_ref, *, tq, tk, nkb, base_off, band_mask, kv_len):
    h = pl.program_id(0)
    d = pl.program_id(1)
    row = lax.broadcasted_iota(jnp.int32, (tq, tk), 0)
    col = lax.broadcasted_iota(jnp.int32, (tq, tk), 1)
    dist = row - col + (base_off - d * tk)
    idx = jnp.clip(dist, -(CHUNK - 1), REL_MAX) + (CHUNK - 1)

    def body(r, acc):
        return jnp.where(idx == r, tab_ref[h, r], acc)

    bias = lax.fori_loop(0, N_REL, body, jnp.zeros((tq, tk), F32))
    if band_mask:
        q_chunk = ((nkb - 1) * tk + row) >> CHUNK_SHIFT
        k_chunk = (d * tk + col) >> CHUNK_SHIFT
        gap = q_chunk - k_chunk
        valid = (gap >= 0) & (gap <= PREV_CHUNKS)
    else:
        valid = (d * tk + col) < kv_len
    o_ref[...] = jnp.where(valid, bias, NEG_INF)


def _band_bias(table, *, tq, tk, nkb, base_off, band_mask, kv_len):
    kern = functools.partial(_band_bias_kernel, tq=tq, tk=tk, nkb=nkb, base_off=base_off,
                             band_mask=band_mask, kv_len=kv_len)
    return pl.pallas_call(
        kern,
        grid=(H_B, nkb),
        in_specs=[pl.BlockSpec(memory_space=pltpu.SMEM)],
        out_specs=pl.BlockSpec((None, None, tq, tk), lambda h, d: (h, d, 0, 0)),
        out_shape=jax.ShapeDtypeStruct((H_B, nkb, tq, tk), F32),
        compiler_params=_params(2),
        name="band_bias",
    )(table)


def _sb_kernel(q_ref, k_ref, v_ref, o_ref, *, tq, tk, q_off):
    i = pl.program_id(2)
    q0 = i * tq + q_off
    n_kb = (q0 + tq - 1 + tk - 1) // tk
    row = lax.broadcasted_iota(jnp.int32, (tq, tk), 0)
    col = lax.broadcasted_iota(jnp.int32, (tq, tk), 1)
    ur = lax.broadcasted_iota(jnp.int32, (tk, tk), 0)
    uc = lax.broadcasted_iota(jnp.int32, (tk, tk), 1)
    later = jnp.where(ur > uc, -1.0, 0.0).astype(BF16)
    outs = []
    for q in _half_masks(q_ref[...]):

        def body(jj, carry, q=q):
            tail0, acc = carry
            j = n_kb - 1 - jj
            start = pl.multiple_of(j * tk, tk)
            z = _dot_nt(q, k_ref[pl.ds(start, tk), :])
            t = jnp.log1p(jnp.exp(-jnp.abs(z)))
            valid = (col + j * tk) < (row + q0)
            sp = jnp.where(valid, jnp.maximum(z, 0.0) + t, 0.0)
            log_sig = jnp.minimum(z, 0.0) - t
            sp_hi = sp.astype(BF16)
            sp_lo = (sp - sp_hi.astype(F32)).astype(BF16)
            tail = _dot(sp_hi, later) + _dot(sp_lo, later) + tail0
            w = jnp.where(valid, jnp.exp(log_sig + tail), 0.0)
            acc = acc + _dot(w.astype(BF16), v_ref[pl.ds(start, tk), :])
            tail0 = tail0 - jnp.sum(sp, axis=-1, keepdims=True)
            return tail0, acc

        init = (jnp.zeros((tq, 1), F32), jnp.zeros((tq, 2 * HEAD_DIM), F32))
        _, acc = lax.fori_loop(0, n_kb, body, init)
        outs.append(acc)
    o_ref[...] = _pair_select(outs[0], outs[1]).astype(o_ref.dtype)


def _sb_attn(q, k, v, *, tq, tk, q_off):
    nb, sq, w = q.shape
    sk = k.shape[1]
    pairs = H_C // 2
    pw = 2 * HEAD_DIM
    assert sk % tk == 0 and sq % tq == 0 and q_off + sq <= sk
    kern = functools.partial(_sb_kernel, tq=tq, tk=tk, q_off=q_off)
    return pl.pallas_call(
        kern,
        grid=(nb, pairs, sq // tq),
        in_specs=[pl.BlockSpec((None, tq, pw), lambda b, p, i: (b, i, p)),
                  pl.BlockSpec((None, sk, pw), lambda b, p, i: (b, 0, p)),
                  pl.BlockSpec((None, sk, pw), lambda b, p, i: (b, 0, p))],
        out_specs=pl.BlockSpec((None, tq, pw), lambda b, p, i: (b, i, p)),
        out_shape=jax.ShapeDtypeStruct((nb, sq, w), BF16),
        compiler_params=_params(3),
        name="sb_attn",
    )(q, k, v)


HC = H_C * HEAD_DIM


def _odd_proj_kernel(x_ref, w_ref, q_ref, k32_ref, v32_ref, k16_ref, v16_ref):
    xb = x_ref[...].astype(BF16)
    h = _dot(xb, w_ref[...])
    q_ref[...] = (h[:, 0:HC] * QK_SCALE).astype(BF16)
    k = h[:, HC:2 * HC]
    v = h[:, 2 * HC:3 * HC]
    k32_ref[...] = k
    v32_ref[...] = v
    k16_ref[...] = k.astype(BF16)
    v16_ref[...] = v.astype(BF16)


def _odd_proj(x, w):
    t, d = x.shape
    tm = min(ROW_TILE, t)
    return pl.pallas_call(
        _odd_proj_kernel,
        grid=(t // tm,),
        in_specs=[_rows(tm, d), _resident(w.shape)],
        out_specs=(_rows(tm, HC),) * 5,
        out_shape=(jax.ShapeDtypeStruct((t, HC), BF16), jax.ShapeDtypeStruct((t, HC), F32),
                   jax.ShapeDtypeStruct((t, HC), F32), jax.ShapeDtypeStruct((t, HC), BF16),
                   jax.ShapeDtypeStruct((t, HC), BF16)),
        compiler_params=_params(1, V7X_VMEM_LIMIT_BYTES),
        name="odd_in_proj",
    )(x, w)


def _out_ln_kernel(*refs, n_mix, alpha):
    x_ref = refs[0]
    mix_refs = refs[1:1 + n_mix]
    w_refs = refs[1 + n_mix:1 + 2 * n_mix]
    g_ref, b_ref, o_ref = refs[1 + 2 * n_mix:]
    y = _dot(mix_refs[0][...], w_refs[0][...])
    for m_ref, w_ref in zip(mix_refs[1:], w_refs[1:]):
        y = y + _dot(m_ref[...], w_ref[...])
    o_ref[...] = _layer_norm(alpha * x_ref[...] + y, g_ref[...], b_ref[...])


def _out_ln(x, mixes, ws, g, b, alpha):
    t, d = x.shape
    tm = min(ROW_TILE, t)
    kern = functools.partial(_out_ln_kernel, n_mix=len(mixes), alpha=alpha)
    return pl.pallas_call(
        kern,
        grid=(t // tm,),
        in_specs=([_rows(tm, d)] + [_rows(tm, m.shape[1]) for m in mixes]
                  + [_resident(w.shape) for w in ws] + [_resident(g.shape), _resident(b.shape)]),
        out_specs=_rows(tm, d),
        out_shape=jax.ShapeDtypeStruct((t, d), F32),
        compiler_params=_params(1),
        name="out_proj_ln",
    )(x, *mixes, *ws, g, b)


def _ffn_ln_kernel(x_ref, wu_ref, wd_ref, g_ref, b_ref, o_ref, *, alpha):
    x = x_ref[...]
    h = _dot(x.astype(BF16), wu_ref[...])
    h = jnp.maximum(h, 0.0)
    ff = _dot((h * h).astype(BF16), wd_ref[...])
    o_ref[...] = _layer_norm(alpha * x + ff, g_ref[...], b_ref[...])


def _ffn_ln(x, wu, wd, g, b, alpha):
    t, d = x.shape
    tm = min(ROW_TILE, t)
    kern = functools.partial(_ffn_ln_kernel, alpha=alpha)
    return pl.pallas_call(
        kern,
        grid=(t // tm,),
        in_specs=[_rows(tm, d), _resident(wu.shape), _resident(wd.shape),
                  _resident(g.shape), _resident(b.shape)],
        out_specs=_rows(tm, d),
        out_shape=jax.ShapeDtypeStruct((t, d), F32),
        compiler_params=_params(1, V7X_VMEM_LIMIT_BYTES),
        name="ffn_ln",
    )(x, wu, wd, g, b)


def _rope_tables(pos, reps):
    inv = jnp.power(ROPE_THETA, -jnp.arange(ROPE_HALF, dtype=F32) / ROPE_HALF)
    ang = pos.astype(F32)[:, None] * inv[None, :]
    cos, sin = jnp.cos(ang), jnp.sin(ang)
    n = pos.shape[0]
    z = lambda w: jnp.zeros((n, w), F32)
    tail = LANES - ROPE_LANE0 - ROPE_DIM
    c = jnp.concatenate([jnp.ones((n, ROPE_LANE0), F32), cos, cos, z(tail)], axis=1)
    s1 = jnp.concatenate([z(ROPE_LANE0), -sin, z(ROPE_HALF), z(tail)], axis=1)
    s2 = jnp.concatenate([z(ROPE_LANE0), z(ROPE_HALF), sin, z(tail)], axis=1)
    return tuple(jnp.tile(a, (reps, 1)) for a in (c, s1, s2))


def _even_weights(w_in, g_q, w_uq, g_kv, w_ukv, w_out):
    d = w_in.shape[0]
    o_r = Q_LORA + KV_LORA
    tail = LANES - ROPE_LANE0 - ROPE_DIM
    w1 = jnp.concatenate(
        [w_in[:, :o_r], jnp.zeros((d, ROPE_LANE0), F32), w_in[:, o_r:o_r + ROPE_DIM],
         jnp.zeros((d, tail), F32), w_in[:, o_r + ROPE_DIM:]], axis=1).astype(BF16)
    per_head = NOPE_DIM + ROPE_DIM
    wuq = jnp.pad(w_uq.reshape(Q_LORA, H_A, per_head),
                  ((0, 0), (0, 0), (0, MLA_HEAD_LANES - per_head)))
    wuq = wuq.reshape(Q_LORA, H_A * MLA_HEAD_LANES).astype(BF16)
    wkv = w_ukv.reshape(KV_LORA, H_A, NOPE_DIM + V_DIM)
    wk = jnp.pad(wkv[:, :, :NOPE_DIM], ((0, 0), (0, 0), (0, MLA_HEAD_LANES - NOPE_DIM)))
    wk = wk.reshape(KV_LORA, H_A * MLA_HEAD_LANES).astype(BF16)
    wv = wkv[:, :, NOPE_DIM:].reshape(KV_LORA, H_A * V_DIM).astype(BF16)
    wo = w_out.astype(BF16)
    wo_a, wo_b = wo[:H_A * V_DIM], wo[H_A * V_DIM:]
    return dict(w1=w1, gq=g_q[None, :], gkv=g_kv[None, :], wuq=wuq, wk=wk, wv=wv,
                wo_a=wo_a, wo_b=wo_b)


def _round_up(n, m):
    return (n + m - 1) // m * m


def _pad_rows(a, rows):
    return jnp.pad(a, ((0, 0), (0, rows - a.shape[1]), (0, 0)))


ATTN_TILE = 256


def _even_mixer(x, nb, t, ropes, w, table, cache):
    qpad, ckv, krw, qb, kb32, vb32, kb16, vb16 = _even_proj(x, w["w1"], w["gq"], w["gkv"], w["wuq"], ropes)
    kpad, v16 = _kv_up(ckv, krw, w["wk"], w["wv"])
    r3 = lambda a: a.reshape(nb, t, a.shape[-1])
    qpad, kpad, v16, qb, kb16, vb16 = map(r3, (qpad, kpad, v16, qb, kb16, vb16))
    kb32 = kb32.reshape(nb, t, H_B, HEAD_DIM)
    vb32 = vb32.reshape(nb, t, H_B, HEAD_DIM)
    if cache is None:
        tq = min(ATTN_TILE, t)
        o_a = _mla_attn(qpad, kpad, v16, tq=tq, tk=tq, causal=True, kv_len=t)
        nkb = min(BAND_PAST // tq + 1, t // tq)
        bias = _band_bias(table, tq=tq, tk=tq, nkb=nkb, base_off=(nkb - 1) * tq,
                          band_mask=True, kv_len=t)
        o_b = _band_attn(qb, kb16, vb16, bias, tq=tq, tk=tq, nkb=nkb)
        rows = min(BAND_PAST, t)
        new_bk, new_bv = kb32[:, t - rows:], vb32[:, t - rows:]
    else:
        c_ckv, c_kr, c_bk, c_bv = cache
        past = c_ckv.shape[1]
        tail = LANES - ROPE_LANE0 - ROPE_DIM
        c_krw = jnp.pad(c_kr, ((0, 0), (0, 0), (ROPE_LANE0, tail)))
        kpad_c, v16_c = _kv_up(c_ckv.reshape(nb * past, KV_LORA), c_krw.reshape(nb * past, LANES),
                               w["wk"], w["wv"])
        kv_len = past + t
        sk = _round_up(kv_len, LANES)
        k_all = _pad_rows(jnp.concatenate([kpad_c.reshape(nb, past, -1), kpad], axis=1), sk)
        v_all = _pad_rows(jnp.concatenate([v16_c.reshape(nb, past, -1), v16], axis=1), sk)
        o_a = _mla_attn(qpad, k_all, v_all, tq=t, tk=sk, causal=False, kv_len=kv_len)
        n_past = c_bk.shape[1]
        bk_all = jnp.concatenate([c_bk, kb32], axis=1)
        bv_all = jnp.concatenate([c_bv, vb32], axis=1)
        band_len = n_past + t
        bsk = _round_up(band_len, LANES)
        flat16 = lambda a: _pad_rows(a.reshape(nb, band_len, HB).astype(BF16), bsk)
        bias = _band_bias(table, tq=t, tk=bsk, nkb=1, base_off=n_past, band_mask=False,
                          kv_len=band_len)
        o_b = _band_attn(qb, flat16(bk_all), flat16(bv_all), bias, tq=t, tk=bsk, nkb=1)
        rows = min(BAND_PAST, band_len)
        new_bk, new_bv = bk_all[:, band_len - rows:], bv_all[:, band_len - rows:]
    mixes = (o_a.reshape(nb * t, -1), o_b.reshape(nb * t, -1))
    new_ckv = ckv.reshape(nb, t, KV_LORA)
    new_kr = krw[:, ROPE_LANE0:ROPE_LANE0 + ROPE_DIM].reshape(nb, t, ROPE_DIM)
    return mixes, (w["wo_a"], w["wo_b"]), (new_ckv, new_kr, new_bk, new_bv)


def _odd_mixer(x, nb, t, w_in, w_out, cache):
    q16, k32, v32, k16, v16 = _odd_proj(x, w_in)
    r3 = lambda a: a.reshape(nb, t, HC)
    q16, k16, v16 = map(r3, (q16, k16, v16))
    if cache is None:
        tq = min(ATTN_TILE, t)
        o = _sb_attn(q16, k16, v16, tq=tq, tk=tq, q_off=0)
    else:
        c_k, c_v = cache
        past = c_k.shape[1]
        tk = 3 * LANES
        sk = _round_up(past + t, tk)
        cat = lambda c, n: _pad_rows(
            jnp.concatenate([c.reshape(nb, past, HC).astype(BF16), n], axis=1), sk)
        o = _sb_attn(q16, cat(c_k, k16), cat(c_v, v16), tq=t, tk=tk, q_off=past)
    new_k = k32.reshape(nb, t, H_C, HEAD_DIM)
    new_v = v32.reshape(nb, t, H_C, HEAD_DIM)
    return (o.reshape(nb * t, HC),), (w_out,), (new_k, new_v)


def kernel(x_prompt, x_sample, cache_mla_ckv, cache_mla_krope, cache_band_k, cache_band_v,
           cache_sb_k, cache_sb_v, w_in_ab, g_q_lat, w_uq, g_kv_lat, w_ukv, rel_bias, w_out_ab,
           w_in_c, w_out_c, ln_mix_g, ln_mix_b, ln_ffn_g, ln_ffn_b, w_ff_up, w_ff_down):
    nb_p, t_p, d = x_prompt.shape
    nb_s, t_s, _ = x_sample.shape
    past = cache_mla_ckv.shape[2]
    depth = ln_mix_g.shape[0]
    alpha = (2.0 * depth) ** 0.25
    xp = x_prompt.reshape(nb_p * t_p, d)
    xs = x_sample.reshape(nb_s * t_s, d)
    ropes_p = _rope_tables(jnp.arange(t_p, dtype=jnp.int32), nb_p)
    ropes_s = _rope_tables(past + jnp.arange(t_s, dtype=jnp.int32), nb_s)
    even_p, even_s, odd_p, odd_s = [], [], [], []
    for l in range(depth):
        i = l // 2
        if l % 2 == 0:
            w = _even_weights(w_in_ab[i], g_q_lat[i], w_uq[i], g_kv_lat[i], w_ukv[i], w_out_ab[i])
            mp, wo, new_p = _even_mixer(xp, nb_p, t_p, ropes_p, w, rel_bias[i], None)
            ms, _, new_s = _even_mixer(
                xs, nb_s, t_s, ropes_s, w, rel_bias[i],
                (cache_mla_ckv[i], cache_mla_krope[i], cache_band_k[i], cache_band_v[i]))
            even_p.append(new_p)
            even_s.append(new_s)
        else:
            w_in = w_in_c[i].astype(BF16)
            w_out = w_out_c[i].astype(BF16)
            mp, wo, new_p = _odd_mixer(xp, nb_p, t_p, w_in, w_out, None)
            ms, _, new_s = _odd_mixer(xs, nb_s, t_s, w_in, w_out, (cache_sb_k[i], cache_sb_v[i]))
            odd_p.append(new_p)
            odd_s.append(new_s)
        g1, b1 = ln_mix_g[l][None, :], ln_mix_b[l][None, :]
        g2, b2 = ln_ffn_g[l][None, :], ln_ffn_b[l][None, :]
        wu, wd = w_ff_up[l].astype(BF16), w_ff_down[l].astype(BF16)
        xp = _ffn_ln(_out_ln(xp, mp, wo, g1, b1, alpha), wu, wd, g2, b2, alpha)
        xs = _ffn_ln(_out_ln(xs, ms, wo, g1, b1, alpha), wu, wd, g2, b2, alpha)
    stack = lambda groups, k: jnp.stack([g[k] for g in groups])
    return (xp.reshape(nb_p, t_p, d), xs.reshape(nb_s, t_s, d),
            stack(even_p, 0), stack(even_p, 1), stack(even_p, 2), stack(even_p, 3),
            stack(odd_p, 0), stack(odd_p, 1),
            stack(even_s, 0), stack(even_s, 1), stack(even_s, 2), stack(even_s, 3),
            stack(odd_s, 0), stack(odd_s, 1))
```

```python
import functools

import jax
import jax.numpy as jnp
from jax import lax
from jax.experimental import pallas as pl
from jax.experimental.pallas import tpu as pltpu

F32 = jnp.float32
BF16 = jnp.bfloat16

CHUNK = 64
CHUNK_SHIFT = 6
HEAD_DIM = 64
H_A = 8
Q_LORA = 768
KV_LORA = 256
NOPE_DIM = 64
ROPE_DIM = 32
V_DIM = 64
ROPE_THETA = 10000.0
MLA_SCALE = (NOPE_DIM + ROPE_DIM) ** -0.5
H_B = 8
PREV_CHUNKS = 8
BAND_PAST = PREV_CHUNKS * CHUNK
REL_MAX = 256
N_REL = REL_MAX + CHUNK
H_C = 16
QK_SCALE = HEAD_DIM ** -0.5
NEG_INF = -1e30

LANES = 128
V7X_VMEM_LIMIT_BYTES = 56 * 1024 * 1024

MLA_HEAD_LANES = LANES
ROPE_LANE0 = NOPE_DIM
ROPE_HALF = ROPE_DIM // 2

ROW_TILE = 512


def _params(n_axes, vmem_bytes=None):
    return pltpu.CompilerParams(
        dimension_semantics=("parallel",) * n_axes,
        vmem_limit_bytes=vmem_bytes,
    )


def _resident(shape):
    nd = len(shape)
    return pl.BlockSpec(shape, lambda *_: (0,) * nd, pipeline_mode=pl.Buffered(1))


def _rows(tm, width):
    return pl.BlockSpec((tm, width), lambda i: (i, 0))


def _dot(a, b):
    return jnp.dot(a, b, preferred_element_type=F32)


def _dot_nt(a, b):
    return lax.dot_general(a, b, (((1,), (1,)), ((), ())), preferred_element_type=F32)


def _rms(h, g, eps=1e-6):
    return h * lax.rsqrt(jnp.mean(h * h, axis=-1, keepdims=True) + eps) * g


def _layer_norm(r, g, b, eps=1e-5):
    mu = jnp.mean(r, axis=-1, keepdims=True)
    d = r - mu
    var = jnp.mean(d * d, axis=-1, keepdims=True)
    return d * lax.rsqrt(var + eps) * g + b


N_BAND = 3 * H_B * HEAD_DIM
W1_Q0, W1_Q1 = 0, Q_LORA
W1_C0, W1_C1 = W1_Q1, W1_Q1 + KV_LORA
W1_R0, W1_R1 = W1_C1, W1_C1 + LANES
W1_B0, W1_B1 = W1_R1, W1_R1 + N_BAND
HB = H_B * HEAD_DIM


def _even_proj_kernel(x_ref, w1_ref, gq_ref, gkv_ref, wuq_ref, c_ref, s1_ref, s2_ref,
                      qpad_ref, ckv_ref, krw_ref, qb_ref, kb32_ref, vb32_ref, kb16_ref, vb16_ref):
    xb = x_ref[...].astype(BF16)
    c = c_ref[...]
    s1 = s1_ref[...]
    s2 = s2_ref[...]

    def rope(v):
        return (v * c + pltpu.roll(v, LANES - ROPE_HALF, 1) * s1
                + pltpu.roll(v, ROPE_HALF, 1) * s2)

    hq = _dot(xb, w1_ref[:, W1_Q0:W1_Q1])
    qn = _rms(hq, gq_ref[...]).astype(BF16)
    qa = _dot(qn, wuq_ref[...])
    for h in range(H_A):
        sl = slice(h * MLA_HEAD_LANES, (h + 1) * MLA_HEAD_LANES)
        qpad_ref[:, sl] = rope(qa[:, sl]).astype(BF16)

    hc = _dot(xb, w1_ref[:, W1_C0:W1_C1])
    ckv_ref[...] = _rms(hc, gkv_ref[...])

    hk = _dot(xb, w1_ref[:, W1_R0:W1_R1])
    krw_ref[...] = rope(hk)

    hb = _dot(xb, w1_ref[:, W1_B0:W1_B1])
    qb_ref[...] = (hb[:, 0:HB] * QK_SCALE).astype(BF16)
    kb = hb[:, HB:2 * HB]
    vb = hb[:, 2 * HB:3 * HB]
    kb32_ref[...] = kb
    vb32_ref[...] = vb
    kb16_ref[...] = kb.astype(BF16)
    vb16_ref[...] = vb.astype(BF16)


def _even_proj(x, w1, gq, gkv, wuq, rope_tabs):
    t, d = x.shape
    tm = min(ROW_TILE, t)
    c, s1, s2 = rope_tabs
    qw = H_A * MLA_HEAD_LANES
    out_shape = (
        jax.ShapeDtypeStruct((t, qw), BF16),
        jax.ShapeDtypeStruct((t, KV_LORA), F32),
        jax.ShapeDtypeStruct((t, LANES), F32),
        jax.ShapeDtypeStruct((t, HB), BF16),
        jax.ShapeDtypeStruct((t, HB), F32),
        jax.ShapeDtypeStruct((t, HB), F32),
        jax.ShapeDtypeStruct((t, HB), BF16),
        jax.ShapeDtypeStruct((t, HB), BF16),
    )
    return pl.pallas_call(
        _even_proj_kernel,
        grid=(t // tm,),
        in_specs=[_rows(tm, d), _resident(w1.shape), _resident(gq.shape), _resident(gkv.shape),
                  _resident(wuq.shape), _rows(tm, LANES), _rows(tm, LANES), _rows(tm, LANES)],
        out_specs=(_rows(tm, qw), _rows(tm, KV_LORA), _rows(tm, LANES), _rows(tm, HB),
                   _rows(tm, HB), _rows(tm, HB), _rows(tm, HB), _rows(tm, HB)),
        out_shape=out_shape,
        compiler_params=_params(1, V7X_VMEM_LIMIT_BYTES),
        name="even_in_proj",
    )(x, w1, gq, gkv, wuq, c, s1, s2)


def _kv_up_kernel(ckv_ref, krw_ref, wk_ref, wv_ref, kpad_ref, v_ref):
    cb = ckv_ref[...].astype(BF16)
    krw = krw_ref[...]
    k = _dot(cb, wk_ref[...])
    for h in range(H_A):
        sl = slice(h * MLA_HEAD_LANES, (h + 1) * MLA_HEAD_LANES)
        kpad_ref[:, sl] = (k[:, sl] + krw).astype(BF16)
    v_ref[...] = _dot(cb, wv_ref[...]).astype(BF16)


def _kv_up(ckv, krw, wk, wv):
    t = ckv.shape[0]
    tm = min(ROW_TILE, t)
    kw = H_A * MLA_HEAD_LANES
    vw = H_A * V_DIM
    return pl.pallas_call(
        _kv_up_kernel,
        grid=(t // tm,),
        in_specs=[_rows(tm, KV_LORA), _rows(tm, LANES), _resident(wk.shape), _resident(wv.shape)],
        out_specs=(_rows(tm, kw), _rows(tm, vw)),
        out_shape=(jax.ShapeDtypeStruct((t, kw), BF16), jax.ShapeDtypeStruct((t, vw), BF16)),
        compiler_params=_params(1),
        name="mla_kv_up",
    )(ckv, krw, wk, wv)


def _pair_select(lo, hi):
    lane = lax.broadcasted_iota(jnp.int32, lo.shape, 1)
    return jnp.where(lane < HEAD_DIM, lo, hi)


def _mla_kernel(q_ref, k_ref, v_ref, o_ref, *, tq, tk, causal, kv_len):
    i = pl.program_id(2)
    sk = k_ref.shape[0]
    if causal:
        n_full = (i * tq) // tk
        n_kb = ((i + 1) * tq + tk - 1) // tk
    else:
        n_full = kv_len // tk
        n_kb = sk // tk
    row = lax.broadcasted_iota(jnp.int32, (tq, tk), 0)
    col = lax.broadcasted_iota(jnp.int32, (tq, tk), 1)
    q_chunk = (row + i * tq) >> CHUNK_SHIFT
    hsls = [slice(hh * MLA_HEAD_LANES, (hh + 1) * MLA_HEAD_LANES) for hh in range(2)]
    qs = [q_ref[:, hsl] for hsl in hsls]

    def block(j, carry, masked):
        start = pl.multiple_of(j * tk, tk)
        vb = v_ref[pl.ds(start, tk), :]
        ss = [_dot_nt(qs[hh], k_ref[pl.ds(start, tk), hsls[hh]]) for hh in range(2)]
        ps, stats = [], []
        for s, (m, l, _) in zip(ss, carry):
            s = s * MLA_SCALE
            if masked:
                kidx = col + j * tk
                if causal:
                    s = jnp.where((kidx >> CHUNK_SHIFT) <= q_chunk, s, NEG_INF)
                else:
                    s = jnp.where(kidx < kv_len, s, NEG_INF)
            m_new = jnp.maximum(m, jnp.max(s, axis=-1, keepdims=True))
            alpha = jnp.exp(m - m_new)
            p = jnp.exp(s - m_new)
            ps.append(p.astype(BF16))
            stats.append((m_new, alpha, alpha * l + jnp.sum(p, axis=-1, keepdims=True)))
        return tuple((m_new, l, alpha * acc + _dot(p, vb))
                     for p, (m_new, alpha, l), (_, _, acc) in zip(ps, stats, carry))

    init = tuple((jnp.full((tq, 1), NEG_INF, F32), jnp.zeros((tq, 1), F32),
                  jnp.zeros((tq, 2 * V_DIM), F32)) for _ in range(2))
    carry = lax.fori_loop(0, n_full, functools.partial(block, masked=False), init)
    carry = lax.fori_loop(n_full, n_kb, functools.partial(block, masked=True), carry)
    outs = [acc / l for _, l, acc in carry]
    o_ref[...] = _pair_select(outs[0], outs[1]).astype(o_ref.dtype)


def _mla_attn(q, k, v, *, tq, tk, causal, kv_len):
    nb, sq, _ = q.shape
    sk = k.shape[1]
    pairs = H_A // 2
    qk_w = 2 * MLA_HEAD_LANES
    v_w = 2 * V_DIM
    kern = functools.partial(_mla_kernel, tq=tq, tk=tk, causal=causal, kv_len=kv_len)
    return pl.pallas_call(
        kern,
        grid=(nb, pairs, sq // tq),
        in_specs=[pl.BlockSpec((None, tq, qk_w), lambda b, p, i: (b, i, p)),
                  pl.BlockSpec((None, sk, qk_w), lambda b, p, i: (b, 0, p)),
                  pl.BlockSpec((None, sk, v_w), lambda b, p, i: (b, 0, p))],
        out_specs=pl.BlockSpec((None, tq, v_w), lambda b, p, i: (b, i, p)),
        out_shape=jax.ShapeDtypeStruct((nb, sq, H_A * V_DIM), BF16),
        compiler_params=_params(3),
        name="mla_attn",
    )(q, k, v)


def _half_masks(x):
    lane = lax.broadcasted_iota(jnp.int32, x.shape, 1)
    zero = jnp.zeros_like(x)
    return jnp.where(lane < HEAD_DIM, x, zero), jnp.where(lane < HEAD_DIM, zero, x)


def _band_kernel(q_ref, k_ref, v_ref, bias_ref, o_ref, *, tq, tk, nkb):
    i = pl.program_id(2)
    outs = []
    for hh, q in enumerate(_half_masks(q_ref[...])):
        scores = []
        starts = []
        for d in range(nkb):
            kbi = i - (nkb - 1) + d
            start = pl.multiple_of(jnp.maximum(kbi, 0) * tk, tk)
            s = _dot_nt(q, k_ref[pl.ds(start, tk), :]) + bias_ref[hh, d]
            scores.append(jnp.where(kbi >= 0, s, NEG_INF))
            starts.append(start)
        m = scores[0].max(axis=-1, keepdims=True)
        for s in scores[1:]:
            m = jnp.maximum(m, s.max(axis=-1, keepdims=True))
        l = jnp.zeros((tq, 1), F32)
        acc = jnp.zeros((tq, 2 * HEAD_DIM), F32)
        for s, start in zip(scores, starts):
            p = jnp.exp(s - m)
            l = l + jnp.sum(p, axis=-1, keepdims=True)
            acc = acc + _dot(p.astype(BF16), v_ref[pl.ds(start, tk), :])
        outs.append(acc / l)
    o_ref[...] = _pair_select(outs[0], outs[1]).astype(o_ref.dtype)


def _band_attn(q, k, v, bias, *, tq, tk, nkb):
    nb, sq, w = q.shape
    sk = k.shape[1]
    pairs = H_B // 2
    pw = 2 * HEAD_DIM
    assert bias.shape == (H_B, nkb, tq, tk)
    assert tq == tk or (nkb == 1 and sq == tq and sk == tk)
    kern = functools.partial(_band_kernel, tq=tq, tk=tk, nkb=nkb)
    return pl.pallas_call(
        kern,
        grid=(nb, pairs, sq // tq),
        in_specs=[pl.BlockSpec((None, tq, pw), lambda b, p, i: (b, i, p)),
                  pl.BlockSpec((None, sk, pw), lambda b, p, i: (b, 0, p)),
                  pl.BlockSpec((None, sk, pw), lambda b, p, i: (b, 0, p)),
                  pl.BlockSpec((2, nkb, tq, tk), lambda b, p, i: (p, 0, 0, 0))],
        out_specs=pl.BlockSpec((None, tq, pw), lambda b, p, i: (b, i, p)),
        out_shape=jax.ShapeDtypeStruct((nb, sq, w), BF16),
        compiler_params=_params(3),
        name="band_attn",
    )(q, k, v, bias)


def _band_bias_kernel(tab_ref, o_ref, *, tq, tk, nkb, base_off, band_mask, kv_len):
    h = pl.program_id(0)
    d = pl.program_id(1)
    row = lax.broadcasted_iota(jnp.int32, (tq, tk), 0)
    col = lax.broadcasted_iota(jnp.int32, (tq, tk), 1)
    dist = row - col + (base_off - d * tk)
    idx = jnp.clip(dist, -(CHUNK - 1), REL_MAX) + (CHUNK - 1)

    def body(r, acc):
        return jnp.where(idx == r, tab_ref[h, r], acc)

    bias = lax.fori_loop(0, N_REL, body, jnp.zeros((tq, tk), F32))
    if band_mask:
        q_chunk = ((nkb - 1) * tk + row) >> CHUNK_SHIFT
        k_chunk = (d * tk + col) >> CHUNK_SHIFT
        gap = q_chunk - k_chunk
        valid = (gap >= 0) & (gap <= PREV_CHUNKS)
    else:
        valid = (d * tk + col) < kv_len
    o_ref[...] = jnp.where(valid, bias, NEG_INF)


def _band_bias(table, *, tq, tk, nkb, base_off, band_mask, kv_len):
    kern = functools.partial(_band_bias_kernel, tq=tq, tk=tk, nkb=nkb, base_off=base_off,
                             band_mask=band_mask, kv_len=kv_len)
    return pl.pallas_call(
        kern,
        grid=(H_B, nkb),
        in_specs=[pl.BlockSpec(memory_space=pltpu.SMEM)],
        out_specs=pl.BlockSpec((None, None, tq, tk), lambda h, d: (h, d, 0, 0)),
        out_shape=jax.ShapeDtypeStruct((H_B, nkb, tq, tk), F32),
        compiler_params=_params(2),
        name="band_bias",
    )(table)


SB_DEAD_TAIL = -110.0
LOG2_E = 1.4426950408889634


def _sb_kernel(q_ref, k_ref, v_ref, o_ref, *, tq, tk, q_off):
    sq = q_ref.shape[0]
    n_pairs = q_ref.shape[1] // (2 * HEAD_DIM)
    pair_lanes = [slice(p * 2 * HEAD_DIM, (p + 1) * 2 * HEAD_DIM) for p in range(n_pairs)]
    row = lax.broadcasted_iota(jnp.int32, (tq, tk), 0)
    col = lax.broadcasted_iota(jnp.int32, (tq, tk), 1)
    ur = lax.broadcasted_iota(jnp.int32, (tk, tk), 0)
    uc = lax.broadcasted_iota(jnp.int32, (tk, tk), 1)
    later = jnp.where(ur > uc, -1.0, 0.0).astype(BF16)
    later2 = jnp.concatenate([later, later], axis=0)

    def tile(i, _):
        q_start = pl.multiple_of(i * tq, tq)
        q0 = i * tq + q_off
        n_full = q0 // tk
        n_kb = (q0 + tq - 1 + tk - 1) // tk
        chains = [(q, sl) for sl in pair_lanes
                  for q in _half_masks(q_ref[pl.ds(q_start, tq), sl])]

        def block(j, carry, masked):
            start = pl.multiple_of(j * tk, tk)
            if masked:
                valid = (col + j * tk) < (row + q0)
            zs = [_dot_nt(q, k_ref[pl.ds(start, tk), sl]) for q, sl in chains]
            sps, log_sigs = [], []
            for z in zs:
                t = jnp.log(1.0 + jnp.exp2(jnp.abs(z) * -LOG2_E))
                sp = jnp.maximum(z, 0.0) + t
                log_sigs.append(jnp.minimum(z, 0.0) - t)
                sps.append(jnp.where(valid, sp, 0.0) if masked else sp)
            tails = []
            for sp, (tail0, _) in zip(sps, carry):
                sp_hi = sp.astype(BF16)
                sp_lo = (sp - sp_hi.astype(F32)).astype(BF16)
                tails.append(_dot(jnp.concatenate([sp_hi, sp_lo], axis=1), later2) + tail0)
            out = []
            for (_, sl), sp, log_sig, tail, (tail0, acc) in zip(chains, sps, log_sigs, tails, carry):
                w = jnp.exp(log_sig + tail)
                if masked:
                    w = jnp.where(valid, w, 0.0)
                acc = acc + _dot(w.astype(BF16), v_ref[pl.ds(start, tk), sl])
                tail0 = tail0 - jnp.sum(sp, axis=-1, keepdims=True)
                out.append((tail0, acc))
            return tuple(out)

        init = tuple((jnp.zeros((tq, 1), F32), jnp.zeros((tq, 2 * HEAD_DIM), F32))
                     for _ in chains)
        carry = lax.fori_loop(
            0, n_kb - n_full, lambda jj, c: block(n_kb - 1 - jj, c, True), init)

        def alive(c):
            return functools.reduce(jnp.maximum, [jnp.max(t0) for t0, _ in c]) > SB_DEAD_TAIL

        def cond(state):
            j, live, _ = state
            return (j >= 0) & live

        def body(state):
            j, _, c = state
            c = block(j, c, False)
            return j - 1, alive(c), c

        _, _, carry = lax.while_loop(cond, body, (n_full - 1, alive(carry), carry))
        for p, sl in enumerate(pair_lanes):
            o_ref[pl.ds(q_start, tq), sl] = _pair_select(
                carry[2 * p][1], carry[2 * p + 1][1]).astype(o_ref.dtype)
        return 0

    lax.fori_loop(0, sq // tq, tile, 0)


SB_PAIRS_PER_STEP = 2


def _sb_attn(q, k, v, *, tq, tk, q_off):
    nb, sq, w = q.shape
    sk = k.shape[1]
    pw = SB_PAIRS_PER_STEP * 2 * HEAD_DIM
    assert sk % tk == 0 and sq % tq == 0 and q_off + sq <= sk and w % pw == 0
    kern = functools.partial(_sb_kernel, tq=tq, tk=tk, q_off=q_off)
    return pl.pallas_call(
        kern,
        grid=(nb, w // pw),
        in_specs=[pl.BlockSpec((None, sq, pw), lambda b, p: (b, 0, p)),
                  pl.BlockSpec((None, sk, pw), lambda b, p: (b, 0, p)),
                  pl.BlockSpec((None, sk, pw), lambda b, p: (b, 0, p))],
        out_specs=pl.BlockSpec((None, sq, pw), lambda b, p: (b, 0, p)),
        out_shape=jax.ShapeDtypeStruct((nb, sq, w), BF16),
        compiler_params=_params(2),
        name="sb_attn",
    )(q, k, v)


HC = H_C * HEAD_DIM


def _odd_proj_kernel(x_ref, w_ref, q_ref, k32_ref, v32_ref, k16_ref, v16_ref):
    xb = x_ref[...].astype(BF16)
    h = _dot(xb, w_ref[...])
    q_ref[...] = (h[:, 0:HC] * QK_SCALE).astype(BF16)
    k = h[:, HC:2 * HC]
    v = h[:, 2 * HC:3 * HC]
    k32_ref[...] = k
    v32_ref[...] = v
    k16_ref[...] = k.astype(BF16)
    v16_ref[...] = v.astype(BF16)


def _odd_proj(x, w):
    t, d = x.shape
    tm = min(ROW_TILE, t)
    return pl.pallas_call(
        _odd_proj_kernel,
        grid=(t // tm,),
        in_specs=[_rows(tm, d), _resident(w.shape)],
        out_specs=(_rows(tm, HC),) * 5,
        out_shape=(jax.ShapeDtypeStruct((t, HC), BF16), jax.ShapeDtypeStruct((t, HC), F32),
                   jax.ShapeDtypeStruct((t, HC), F32), jax.ShapeDtypeStruct((t, HC), BF16),
                   jax.ShapeDtypeStruct((t, HC), BF16)),
        compiler_params=_params(1, V7X_VMEM_LIMIT_BYTES),
        name="odd_in_proj",
    )(x, w)


def _out_ln_kernel(*refs, n_mix, alpha):
    x_ref = refs[0]
    mix_refs = refs[1:1 + n_mix]
    w_refs = refs[1 + n_mix:1 + 2 * n_mix]
    g_ref, b_ref, o_ref = refs[1 + 2 * n_mix:]
    y = _dot(mix_refs[0][...], w_refs[0][...])
    for m_ref, w_ref in zip(mix_refs[1:], w_refs[1:]):
        y = y + _dot(m_ref[...], w_ref[...])
    o_ref[...] = _layer_norm(alpha * x_ref[...] + y, g_ref[...], b_ref[...])


def _out_ln(x, mixes, ws, g, b, alpha):
    t, d = x.shape
    tm = min(ROW_TILE, t)
    kern = functools.partial(_out_ln_kernel, n_mix=len(mixes), alpha=alpha)
    return pl.pallas_call(
        kern,
        grid=(t // tm,),
        in_specs=([_rows(tm, d)] + [_rows(tm, m.shape[1]) for m in mixes]
                  + [_resident(w.shape) for w in ws] + [_resident(g.shape), _resident(b.shape)]),
        out_specs=_rows(tm, d),
        out_shape=jax.ShapeDtypeStruct((t, d), F32),
        compiler_params=_params(1),
        name="out_proj_ln",
    )(x, *mixes, *ws, g, b)


def _ffn_ln_kernel(x_ref, wu_ref, wd_ref, g_ref, b_ref, o_ref, *, alpha):
    x = x_ref[...]
    h = _dot(x.astype(BF16), wu_ref[...])
    h = jnp.maximum(h, 0.0)
    ff = _dot((h * h).astype(BF16), wd_ref[...])
    o_ref[...] = _layer_norm(alpha * x + ff, g_ref[...], b_ref[...])


def _ffn_ln(x, wu, wd, g, b, alpha):
    t, d = x.shape
    tm = min(ROW_TILE, t)
    kern = functools.partial(_ffn_ln_kernel, alpha=alpha)
    return pl.pallas_call(
        kern,
        grid=(t // tm,),
        in_specs=[_rows(tm, d), _resident(wu.shape), _resident(wd.shape),
                  _resident(g.shape), _resident(b.shape)],
        out_specs=_rows(tm, d),
        out_shape=jax.ShapeDtypeStruct((t, d), F32),
        compiler_params=_params(1, V7X_VMEM_LIMIT_BYTES),
        name="ffn_ln",
    )(x, wu, wd, g, b)


def _rope_tables(pos, reps):
    inv = jnp.power(ROPE_THETA, -jnp.arange(ROPE_HALF, dtype=F32) / ROPE_HALF)
    ang = pos.astype(F32)[:, None] * inv[None, :]
    cos, sin = jnp.cos(ang), jnp.sin(ang)
    n = pos.shape[0]
    z = lambda w: jnp.zeros((n, w), F32)
    tail = LANES - ROPE_LANE0 - ROPE_DIM
    c = jnp.concatenate([jnp.ones((n, ROPE_LANE0), F32), cos, cos, z(tail)], axis=1)
    s1 = jnp.concatenate([z(ROPE_LANE0), -sin, z(ROPE_HALF), z(tail)], axis=1)
    s2 = jnp.concatenate([z(ROPE_LANE0), z(ROPE_HALF), sin, z(tail)], axis=1)
    return tuple(jnp.tile(a, (reps, 1)) for a in (c, s1, s2))


def _even_weights(w_in, g_q, w_uq, g_kv, w_ukv, w_out):
    d = w_in.shape[0]
    o_r = Q_LORA + KV_LORA
    tail = LANES - ROPE_LANE0 - ROPE_DIM
    w1 = jnp.concatenate(
        [w_in[:, :o_r], jnp.zeros((d, ROPE_LANE0), F32), w_in[:, o_r:o_r + ROPE_DIM],
         jnp.zeros((d, tail), F32), w_in[:, o_r + ROPE_DIM:]], axis=1).astype(BF16)
    per_head = NOPE_DIM + ROPE_DIM
    wuq = jnp.pad(w_uq.reshape(Q_LORA, H_A, per_head),
                  ((0, 0), (0, 0), (0, MLA_HEAD_LANES - per_head)))
    wuq = wuq.reshape(Q_LORA, H_A * MLA_HEAD_LANES).astype(BF16)
    wkv = w_ukv.reshape(KV_LORA, H_A, NOPE_DIM + V_DIM)
    wk = jnp.pad(wkv[:, :, :NOPE_DIM], ((0, 0), (0, 0), (0, MLA_HEAD_LANES - NOPE_DIM)))
    wk = wk.reshape(KV_LORA, H_A * MLA_HEAD_LANES).astype(BF16)
    wv = wkv[:, :, NOPE_DIM:].reshape(KV_LORA, H_A * V_DIM).astype(BF16)
    wo = w_out.astype(BF16)
    wo_a, wo_b = wo[:H_A * V_DIM], wo[H_A * V_DIM:]
    return dict(w1=w1, gq=g_q[None, :], gkv=g_kv[None, :], wuq=wuq, wk=wk, wv=wv,
                wo_a=wo_a, wo_b=wo_b)


def _round_up(n, m):
    return (n + m - 1) // m * m


def _pad_rows(a, rows):
    return jnp.pad(a, ((0, 0), (0, rows - a.shape[1]), (0, 0)))


ATTN_TILE = 256


def _even_mixer(x, nb, t, ropes, w, table, cache):
    qpad, ckv, krw, qb, kb32, vb32, kb16, vb16 = _even_proj(x, w["w1"], w["gq"], w["gkv"], w["wuq"], ropes)
    kpad, v16 = _kv_up(ckv, krw, w["wk"], w["wv"])
    r3 = lambda a: a.reshape(nb, t, a.shape[-1])
    qpad, kpad, v16, qb, kb16, vb16 = map(r3, (qpad, kpad, v16, qb, kb16, vb16))
    kb32 = kb32.reshape(nb, t, H_B, HEAD_DIM)
    vb32 = vb32.reshape(nb, t, H_B, HEAD_DIM)
    if cache is None:
        tq = min(ATTN_TILE, t)
        o_a = _mla_attn(qpad, kpad, v16, tq=tq, tk=min(2 * tq, t), causal=True, kv_len=t)
        nkb = min(BAND_PAST // tq + 1, t // tq)
        bias = _band_bias(table, tq=tq, tk=tq, nkb=nkb, base_off=(nkb - 1) * tq,
                          band_mask=True, kv_len=t)
        o_b = _band_attn(qb, kb16, vb16, bias, tq=tq, tk=tq, nkb=nkb)
        rows = min(BAND_PAST, t)
        new_bk, new_bv = kb32[:, t - rows:], vb32[:, t - rows:]
    else:
        c_ckv, c_kr, c_bk, c_bv = cache
        past = c_ckv.shape[1]
        tail = LANES - ROPE_LANE0 - ROPE_DIM
        c_krw = jnp.pad(c_kr, ((0, 0), (0, 0), (ROPE_LANE0, tail)))
        kpad_c, v16_c = _kv_up(c_ckv.reshape(nb * past, KV_LORA), c_krw.reshape(nb * past, LANES),
                               w["wk"], w["wv"])
        kv_len = past + t
        sk = _round_up(kv_len, LANES)
        k_all = _pad_rows(jnp.concatenate([kpad_c.reshape(nb, past, -1), kpad], axis=1), sk)
        v_all = _pad_rows(jnp.concatenate([v16_c.reshape(nb, past, -1), v16], axis=1), sk)
        o_a = _mla_attn(qpad, k_all, v_all, tq=t, tk=sk, causal=False, kv_len=kv_len)
        n_past = c_bk.shape[1]
        bk_all = jnp.concatenate([c_bk, kb32], axis=1)
        bv_all = jnp.concatenate([c_bv, vb32], axis=1)
        band_len = n_past + t
        bsk = _round_up(band_len, LANES)
        flat16 = lambda a: _pad_rows(a.reshape(nb, band_len, HB).astype(BF16), bsk)
        bias = _band_bias(table, tq=t, tk=bsk, nkb=1, base_off=n_past, band_mask=False,
                          kv_len=band_len)
        o_b = _band_attn(qb, flat16(bk_all), flat16(bv_all), bias, tq=t, tk=bsk, nkb=1)
        rows = min(BAND_PAST, band_len)
        new_bk, new_bv = bk_all[:, band_len - rows:], bv_all[:, band_len - rows:]
    mixes = (o_a.reshape(nb * t, -1), o_b.reshape(nb * t, -1))
    new_ckv = ckv.reshape(nb, t, KV_LORA)
    new_kr = krw[:, ROPE_LANE0:ROPE_LANE0 + ROPE_DIM].reshape(nb, t, ROPE_DIM)
    return mixes, (w["wo_a"], w["wo_b"]), (new_ckv, new_kr, new_bk, new_bv)


def _odd_mixer(x, nb, t, w_in, w_out, cache):
    q16, k32, v32, k16, v16 = _odd_proj(x, w_in)
    r3 = lambda a: a.reshape(nb, t, HC)
    q16, k16, v16 = map(r3, (q16, k16, v16))
    if cache is None:
        tq = min(ATTN_TILE, t)
        o = _sb_attn(q16, k16, v16, tq=tq, tk=tq, q_off=0)
    else:
        c_k, c_v = cache
        past = c_k.shape[1]
        tk = 3 * LANES
        sk = _round_up(past + t, tk)
        cat = lambda c, n: _pad_rows(
            jnp.concatenate([c.reshape(nb, past, HC).astype(BF16), n], axis=1), sk)
        o = _sb_attn(q16, cat(c_k, k16), cat(c_v, v16), tq=t, tk=tk, q_off=past)
    new_k = k32.reshape(nb, t, H_C, HEAD_DIM)
    new_v = v32.reshape(nb, t, H_C, HEAD_DIM)
    return (o.reshape(nb * t, HC),), (w_out,), (new_k, new_v)


def kernel(x_prompt, x_sample, cache_mla_ckv, cache_mla_krope, cache_band_k, cache_band_v,
           cache_sb_k, cache_sb_v, w_in_ab, g_q_lat, w_uq, g_kv_lat, w_ukv, rel_bias, w_out_ab,
           w_in_c, w_out_c, ln_mix_g, ln_mix_b, ln_ffn_g, ln_ffn_b, w_ff_up, w_ff_down):
    nb_p, t_p, d = x_prompt.shape
    nb_s, t_s, _ = x_sample.shape
    past = cache_mla_ckv.shape[2]
    depth = ln_mix_g.shape[0]
    alpha = (2.0 * depth) ** 0.25
    xp = x_prompt.reshape(nb_p * t_p, d)
    xs = x_sample.reshape(nb_s * t_s, d)
    ropes_p = _rope_tables(jnp.arange(t_p, dtype=jnp.int32), nb_p)
    ropes_s = _rope_tables(past + jnp.arange(t_s, dtype=jnp.int32), nb_s)
    even_p, even_s, odd_p, odd_s = [], [], [], []
    for l in range(depth):
        i = l // 2
        if l % 2 == 0:
            w = _even_weights(w_in_ab[i], g_q_lat[i], w_uq[i], g_kv_lat[i], w_ukv[i], w_out_ab[i])
            mp, wo, new_p = _even_mixer(xp, nb_p, t_p, ropes_p, w, rel_bias[i], None)
            ms, _, new_s = _even_mixer(
                xs, nb_s, t_s, ropes_s, w, rel_bias[i],
                (cache_mla_ckv[i], cache_mla_krope[i], cache_band_k[i], cache_band_v[i]))
            even_p.append(new_p)
            even_s.append(new_s)
        else:
            w_in = w_in_c[i].astype(BF16)
            w_out = w_out_c[i].astype(BF16)
            mp, wo, new_p = _odd_mixer(xp, nb_p, t_p, w_in, w_out, None)
            ms, _, new_s = _odd_mixer(xs, nb_s, t_s, w_in, w_out, (cache_sb_k[i], cache_sb_v[i]))
            odd_p.append(new_p)
            odd_s.append(new_s)
        g1, b1 = ln_mix_g[l][None, :], ln_mix_b[l][None, :]
        g2, b2 = ln_ffn_g[l][None, :], ln_ffn_b[l][None, :]
        wu, wd = w_ff_up[l].astype(BF16), w_ff_down[l].astype(BF16)
        xp = _ffn_ln(_out_ln(xp, mp, wo, g1, b1, alpha), wu, wd, g2, b2, alpha)
        xs = _ffn_ln(_out_ln(xs, ms, wo, g1, b1, alpha), wu, wd, g2, b2, alpha)
    stack = lambda groups, k: jnp.stack([g[k] for g in groups])
    return (xp.reshape(nb_p, t_p, d), xs.reshape(nb_s, t_s, d),
            stack(even_p, 0), stack(even_p, 1), stack(even_p, 2), stack(even_p, 3),
            stack(odd_p, 0), stack(odd_p, 1),
            stack(even_s, 0), stack(even_s, 1), stack(even_s, 2), stack(even_s, 3),
            stack(odd_s, 0), stack(odd_s, 1))
```

```python
import functools

import jax
import jax.numpy as jnp
from jax import lax
from jax.experimental import pallas as pl
from jax.experimental.pallas import tpu as pltpu

F32 = jnp.float32
BF16 = jnp.bfloat16

CHUNK = 64
CHUNK_SHIFT = 6
HEAD_DIM = 64
H_A = 8
Q_LORA = 768
KV_LORA = 256
NOPE_DIM = 64
ROPE_DIM = 32
V_DIM = 64
ROPE_THETA = 10000.0
MLA_SCALE = (NOPE_DIM + ROPE_DIM) ** -0.5
H_B = 8
PREV_CHUNKS = 8
BAND_PAST = PREV_CHUNKS * CHUNK
REL_MAX = 256
N_REL = REL_MAX + CHUNK
H_C = 16
QK_SCALE = HEAD_DIM ** -0.5
NEG_INF = -1e30

LANES = 128
SUBLANES = 8
V7X_VMEM_LIMIT_BYTES = 56 * 1024 * 1024

MLA_HEAD_LANES = LANES
ROPE_LANE0 = NOPE_DIM
ROPE_HALF = ROPE_DIM // 2

ROW_TILE = 512


def _params(n_axes, vmem_bytes=None):
    return pltpu.CompilerParams(
        dimension_semantics=("parallel",) * n_axes,
        vmem_limit_bytes=vmem_bytes,
    )


def _resident(shape):
    nd = len(shape)
    return pl.BlockSpec(shape, lambda *_: (0,) * nd, pipeline_mode=pl.Buffered(1))


def _rows(tm, width):
    return pl.BlockSpec((tm, width), lambda i: (i, 0))


def _dot(a, b):
    return jnp.dot(a, b, preferred_element_type=F32)


def _dot_nt(a, b):
    return lax.dot_general(a, b, (((1,), (1,)), ((), ())), preferred_element_type=F32)


def _rms(h, g, eps=1e-6):
    return h * lax.rsqrt(jnp.mean(h * h, axis=-1, keepdims=True) + eps) * g


def _layer_norm(r, g, b, eps=1e-5):
    mu = jnp.mean(r, axis=-1, keepdims=True)
    d = r - mu
    var = jnp.mean(d * d, axis=-1, keepdims=True)
    return d * lax.rsqrt(var + eps) * g + b


N_BAND = 3 * H_B * HEAD_DIM
W1_Q0, W1_Q1 = 0, Q_LORA
W1_C0, W1_C1 = W1_Q1, W1_Q1 + KV_LORA
W1_R0, W1_R1 = W1_C1, W1_C1 + LANES
W1_B0, W1_B1 = W1_R1, W1_R1 + N_BAND
HB = H_B * HEAD_DIM


def _even_proj_kernel(x_ref, w1_ref, gq_ref, gkv_ref, wuq_ref, c_ref, s1_ref, s2_ref,
                      qpad_ref, ckv_ref, krw_ref, qb_ref, kb32_ref, vb32_ref, kb16_ref, vb16_ref):
    xb = x_ref[...].astype(BF16)
    c = c_ref[...]
    s1 = s1_ref[...]
    s2 = s2_ref[...]

    def rope(v):
        return (v * c + pltpu.roll(v, LANES - ROPE_HALF, 1) * s1
                + pltpu.roll(v, ROPE_HALF, 1) * s2)

    hq = _dot(xb, w1_ref[:, W1_Q0:W1_Q1])
    qn = _rms(hq, gq_ref[...]).astype(BF16)
    qa = _dot(qn, wuq_ref[...])
    for h in range(H_A):
        sl = slice(h * MLA_HEAD_LANES, (h + 1) * MLA_HEAD_LANES)
        qpad_ref[:, sl] = rope(qa[:, sl]).astype(BF16)

    hc = _dot(xb, w1_ref[:, W1_C0:W1_C1])
    ckv_ref[...] = _rms(hc, gkv_ref[...])

    hk = _dot(xb, w1_ref[:, W1_R0:W1_R1])
    krw_ref[...] = rope(hk)

    hb = _dot(xb, w1_ref[:, W1_B0:W1_B1])
    qb_ref[...] = (hb[:, 0:HB] * QK_SCALE).astype(BF16)
    kb = hb[:, HB:2 * HB]
    vb = hb[:, 2 * HB:3 * HB]
    kb32_ref[...] = kb
    vb32_ref[...] = vb
    kb16_ref[...] = kb.astype(BF16)
    vb16_ref[...] = vb.astype(BF16)


def _even_proj(x, w1, gq, gkv, wuq, rope_tabs):
    t, d = x.shape
    tm = min(ROW_TILE, t)
    c, s1, s2 = rope_tabs
    qw = H_A * MLA_HEAD_LANES
    out_shape = (
        jax.ShapeDtypeStruct((t, qw), BF16),
        jax.ShapeDtypeStruct((t, KV_LORA), F32),
        jax.ShapeDtypeStruct((t, LANES), F32),
        jax.ShapeDtypeStruct((t, HB), BF16),
        jax.ShapeDtypeStruct((t, HB), F32),
        jax.ShapeDtypeStruct((t, HB), F32),
        jax.ShapeDtypeStruct((t, HB), BF16),
        jax.ShapeDtypeStruct((t, HB), BF16),
    )
    return pl.pallas_call(
        _even_proj_kernel,
        grid=(t // tm,),
        in_specs=[_rows(tm, d), _resident(w1.shape), _resident(gq.shape), _resident(gkv.shape),
                  _resident(wuq.shape), _rows(tm, LANES), _rows(tm, LANES), _rows(tm, LANES)],
        out_specs=(_rows(tm, qw), _rows(tm, KV_LORA), _rows(tm, LANES), _rows(tm, HB),
                   _rows(tm, HB), _rows(tm, HB), _rows(tm, HB), _rows(tm, HB)),
        out_shape=out_shape,
        compiler_params=_params(1, V7X_VMEM_LIMIT_BYTES),
        name="even_in_proj",
    )(x, w1, gq, gkv, wuq, c, s1, s2)


def _kv_up_kernel(ckv_ref, krw_ref, wk_ref, wv_ref, kpad_ref, v_ref):
    cb = ckv_ref[...].astype(BF16)
    krw = krw_ref[...]
    k = _dot(cb, wk_ref[...])
    for h in range(H_A):
        sl = slice(h * MLA_HEAD_LANES, (h + 1) * MLA_HEAD_LANES)
        kpad_ref[:, sl] = (k[:, sl] + krw).astype(BF16)
    v_ref[...] = _dot(cb, wv_ref[...]).astype(BF16)


def _kv_up(ckv, krw, wk, wv):
    t = ckv.shape[0]
    tm = min(ROW_TILE, t)
    kw = H_A * MLA_HEAD_LANES
    vw = H_A * V_DIM
    return pl.pallas_call(
        _kv_up_kernel,
        grid=(t // tm,),
        in_specs=[_rows(tm, KV_LORA), _rows(tm, LANES), _resident(wk.shape), _resident(wv.shape)],
        out_specs=(_rows(tm, kw), _rows(tm, vw)),
        out_shape=(jax.ShapeDtypeStruct((t, kw), BF16), jax.ShapeDtypeStruct((t, vw), BF16)),
        compiler_params=_params(1),
        name="mla_kv_up",
    )(ckv, krw, wk, wv)


def _pair_select(lo, hi):
    lane = lax.broadcasted_iota(jnp.int32, lo.shape, 1)
    return jnp.where(lane < HEAD_DIM, lo, hi)


def _mla_kernel(q_ref, k_ref, v_ref, o_ref, *, tq, tk, causal, kv_len):
    i = pl.program_id(2)
    sk = k_ref.shape[0]
    n_heads = q_ref.shape[1] // MLA_HEAD_LANES
    if causal:
        n_full = (i * tq) // tk
        n_kb = ((i + 1) * tq + tk - 1) // tk
    else:
        n_full = kv_len // tk
        n_kb = sk // tk
    row = lax.broadcasted_iota(jnp.int32, (tq, tk), 0)
    col = lax.broadcasted_iota(jnp.int32, (tq, tk), 1)
    q_chunk = (row + i * tq) >> CHUNK_SHIFT
    hsls = [slice(h * MLA_HEAD_LANES, (h + 1) * MLA_HEAD_LANES) for h in range(n_heads)]
    vsls = [slice(h // 2 * 2 * V_DIM, (h // 2 + 1) * 2 * V_DIM) for h in range(n_heads)]
    qs = [q_ref[:, hsl] for hsl in hsls]

    def block(j, carry, masked):
        start = pl.multiple_of(j * tk, tk)
        ss = [_dot_nt(q, k_ref[pl.ds(start, tk), hsl]) for q, hsl in zip(qs, hsls)]
        ps, stats = [], []
        for s, (m, l, _) in zip(ss, carry):
            if masked:
                kidx = col + j * tk
                if causal:
                    s = jnp.where((kidx >> CHUNK_SHIFT) <= q_chunk, s, NEG_INF)
                else:
                    s = jnp.where(kidx < kv_len, s, NEG_INF)
            m_new = jnp.maximum(m, jnp.max(s, axis=-1, keepdims=True))
            alpha = jnp.exp2((m - m_new) * (MLA_SCALE * LOG2_E))
            p = jnp.exp2((s - m_new) * (MLA_SCALE * LOG2_E))
            ps.append(p.astype(BF16))
            stats.append((m_new, alpha, alpha * l + jnp.sum(p, axis=-1, keepdims=True)))
        return tuple(
            (m_new, l, alpha * acc + _dot(p, v_ref[pl.ds(start, tk), vsl]))
            for p, vsl, (m_new, alpha, l), (_, _, acc) in zip(ps, vsls, stats, carry))

    init = tuple((jnp.full((tq, 1), NEG_INF, F32), jnp.zeros((tq, 1), F32),
                  jnp.zeros((tq, 2 * V_DIM), F32)) for _ in range(n_heads))
    carry = lax.fori_loop(0, n_full, functools.partial(block, masked=False), init)
    carry = lax.fori_loop(n_full, n_kb, functools.partial(block, masked=True), carry)
    outs = [acc / l for _, l, acc in carry]
    for p in range(n_heads // 2):
        o_ref[:, vsls[2 * p]] = _pair_select(outs[2 * p], outs[2 * p + 1]).astype(o_ref.dtype)


MLA_PAIRS_PER_STEP = 2


def _mla_attn(q, k, v, *, tq, tk, causal, kv_len):
    nb, sq, _ = q.shape
    sk = k.shape[1]
    pairs = H_A // (2 * MLA_PAIRS_PER_STEP)
    qk_w = MLA_PAIRS_PER_STEP * 2 * MLA_HEAD_LANES
    v_w = MLA_PAIRS_PER_STEP * 2 * V_DIM
    kern = functools.partial(_mla_kernel, tq=tq, tk=tk, causal=causal, kv_len=kv_len)
    return pl.pallas_call(
        kern,
        grid=(nb, pairs, sq // tq),
        in_specs=[pl.BlockSpec((None, tq, qk_w), lambda b, p, i: (b, i, p)),
                  pl.BlockSpec((None, sk, qk_w), lambda b, p, i: (b, 0, p)),
                  pl.BlockSpec((None, sk, v_w), lambda b, p, i: (b, 0, p))],
        out_specs=pl.BlockSpec((None, tq, v_w), lambda b, p, i: (b, i, p)),
        out_shape=jax.ShapeDtypeStruct((nb, sq, H_A * V_DIM), BF16),
        compiler_params=_params(3),
        name="mla_attn",
    )(q, k, v)


def _half_masks(x):
    lane = lax.broadcasted_iota(jnp.int32, x.shape, 1)
    zero = jnp.zeros_like(x)
    return jnp.where(lane < HEAD_DIM, x, zero), jnp.where(lane < HEAD_DIM, zero, x)


def _band_kernel(q_ref, k_ref, v_ref, bias_ref, o_ref, *, tq, tk, nkb):
    i = pl.program_id(2)
    n_pairs = q_ref.shape[1] // (2 * HEAD_DIM)
    pair_lanes = [slice(p * 2 * HEAD_DIM, (p + 1) * 2 * HEAD_DIM) for p in range(n_pairs)]
    chains = [(2 * p + hh, q, sl) for p, sl in enumerate(pair_lanes)
              for hh, q in enumerate(_half_masks(q_ref[:, sl]))]
    kbis = [i - (nkb - 1) + d for d in range(nkb)]
    starts = [pl.multiple_of(jnp.maximum(kbi, 0) * tk, tk) for kbi in kbis]
    raw = [[_dot_nt(q, k_ref[pl.ds(start, tk), sl]) for start in starts] for _, q, sl in chains]
    probs, sums = [], []
    for (h, _, _), head_raw in zip(chains, raw):
        scores = [jnp.where(kbi >= 0, s + bias_ref[h, d], NEG_INF)
                  for d, (kbi, s) in enumerate(zip(kbis, head_raw))]
        m = functools.reduce(jnp.maximum, [s.max(axis=-1, keepdims=True) for s in scores])
        ps = [jnp.exp(s - m) for s in scores]
        sums.append(functools.reduce(jnp.add, [p.sum(axis=-1, keepdims=True) for p in ps]))
        probs.append([p.astype(BF16) for p in ps])
    outs = []
    for (_, _, sl), ps, l in zip(chains, probs, sums):
        acc = functools.reduce(
            jnp.add, [_dot(p, v_ref[pl.ds(start, tk), sl]) for p, start in zip(ps, starts)])
        outs.append(acc / l)
    for p, sl in enumerate(pair_lanes):
        o_ref[:, sl] = _pair_select(outs[2 * p], outs[2 * p + 1]).astype(o_ref.dtype)


BAND_PAIRS_PER_STEP = 2


def _band_attn(q, k, v, bias, *, tq, tk, nkb):
    nb, sq, w = q.shape
    sk = k.shape[1]
    pw = BAND_PAIRS_PER_STEP * 2 * HEAD_DIM
    heads_per_step = 2 * BAND_PAIRS_PER_STEP
    assert bias.shape == (H_B, nkb, tq, tk) and w % pw == 0
    assert tq == tk or (nkb == 1 and sq == tq and sk == tk)
    kern = functools.partial(_band_kernel, tq=tq, tk=tk, nkb=nkb)
    return pl.pallas_call(
        kern,
        grid=(nb, w // pw, sq // tq),
        in_specs=[pl.BlockSpec((None, tq, pw), lambda b, p, i: (b, i, p)),
                  pl.BlockSpec((None, sk, pw), lambda b, p, i: (b, 0, p)),
                  pl.BlockSpec((None, sk, pw), lambda b, p, i: (b, 0, p)),
                  pl.BlockSpec((heads_per_step, nkb, tq, tk), lambda b, p, i: (p, 0, 0, 0))],
        out_specs=pl.BlockSpec((None, tq, pw), lambda b, p, i: (b, i, p)),
        out_shape=jax.ShapeDtypeStruct((nb, sq, w), BF16),
        compiler_params=_params(3),
        name="band_attn",
    )(q, k, v, bias)


def _band_bias_kernel(tab_ref, o_ref, *, tq, tk, nkb, base_off, band_mask, kv_len):
    h = pl.program_id(0)
    d = pl.program_id(1)
    row = lax.broadcasted_iota(jnp.int32, (tq, tk), 0)
    col = lax.broadcasted_iota(jnp.int32, (tq, tk), 1)
    width = _round_up(tq + tk - 1, LANES)
    u = lax.broadcasted_iota(jnp.int32, (SUBLANES, width), 1)
    col_minus_row = jnp.where(u < tk, u, u - width)
    dist = (base_off - d * tk) - col_minus_row
    idx = jnp.clip(dist, -(CHUNK - 1), REL_MAX) + (CHUNK - 1)

    def body(r, acc):
        return jnp.where(idx == r, tab_ref[h, r], acc)

    line = lax.fori_loop(0, N_REL, body, jnp.zeros((SUBLANES, width), F32))
    spread = jnp.broadcast_to(line[0:1, :], (tq, width))
    bias = pltpu.roll(spread, 0, 1, stride=1, stride_axis=0)[:, :tk]
    if band_mask:
        q_chunk = ((nkb - 1) * tk + row) >> CHUNK_SHIFT
        k_chunk = (d * tk + col) >> CHUNK_SHIFT
        gap = q_chunk - k_chunk
        valid = (gap >= 0) & (gap <= PREV_CHUNKS)
    else:
        valid = (d * tk + col) < kv_len
    o_ref[...] = jnp.where(valid, bias, NEG_INF)


def _band_bias(table, *, tq, tk, nkb, base_off, band_mask, kv_len):
    kern = functools.partial(_band_bias_kernel, tq=tq, tk=tk, nkb=nkb, base_off=base_off,
                             band_mask=band_mask, kv_len=kv_len)
    return pl.pallas_call(
        kern,
        grid=(H_B, nkb),
        in_specs=[pl.BlockSpec(memory_space=pltpu.SMEM)],
        out_specs=pl.BlockSpec((None, None, tq, tk), lambda h, d: (h, d, 0, 0)),
        out_shape=jax.ShapeDtypeStruct((H_B, nkb, tq, tk), F32),
        compiler_params=_params(2),
        name="band_bias",
    )(table)


SB_DEAD_TAIL = -110.0
LOG2_E = 1.4426950408889634


def _sb_kernel(q_ref, k_ref, v_ref, o_ref, *, tq, tk, q_off):
    sq = q_ref.shape[0]
    n_pairs = q_ref.shape[1] // (2 * HEAD_DIM)
    pair_lanes = [slice(p * 2 * HEAD_DIM, (p + 1) * 2 * HEAD_DIM) for p in range(n_pairs)]
    row = lax.broadcasted_iota(jnp.int32, (tq, tk), 0)
    col = lax.broadcasted_iota(jnp.int32, (tq, tk), 1)
    ur = lax.broadcasted_iota(jnp.int32, (tk, tk), 0)
    uc = lax.broadcasted_iota(jnp.int32, (tk, tk), 1)
    later = jnp.where(ur > uc, -1.0, 0.0).astype(BF16)
    later2 = jnp.concatenate([later, later], axis=0)

    def tile(i, _):
        q_start = pl.multiple_of(i * tq, tq)
        q0 = i * tq + q_off
        n_full = q0 // tk
        n_kb = (q0 + tq - 1 + tk - 1) // tk
        chains = [(q, sl) for sl in pair_lanes
                  for q in _half_masks(q_ref[pl.ds(q_start, tq), sl])]

        def block(j, carry, masked):
            start = pl.multiple_of(j * tk, tk)
            if masked:
                valid = (col + j * tk) < (row + q0)
            zs = [_dot_nt(q, k_ref[pl.ds(start, tk), sl]) for q, sl in chains]
            sps, log_sigs = [], []
            for z in zs:
                t = jnp.log(1.0 + jnp.exp2(jnp.abs(z) * -LOG2_E))
                sp = jnp.maximum(z, 0.0) + t
                log_sigs.append(jnp.minimum(z, 0.0) - t)
                sps.append(jnp.where(valid, sp, 0.0) if masked else sp)
            tails = []
            for sp, (tail0, _) in zip(sps, carry):
                sp_hi = sp.astype(BF16)
                sp_lo = (sp - sp_hi.astype(F32)).astype(BF16)
                tails.append(_dot(jnp.concatenate([sp_hi, sp_lo], axis=1), later2) + tail0)
            out = []
            for (_, sl), sp, log_sig, tail, (tail0, acc) in zip(chains, sps, log_sigs, tails, carry):
                w = jnp.exp(log_sig + tail)
                if masked:
                    w = jnp.where(valid, w, 0.0)
                acc = acc + _dot(w.astype(BF16), v_ref[pl.ds(start, tk), sl])
                tail0 = tail0 - jnp.sum(sp, axis=-1, keepdims=True)
                out.append((tail0, acc))
            return tuple(out)

        init = tuple((jnp.zeros((tq, 1), F32), jnp.zeros((tq, 2 * HEAD_DIM), F32))
                     for _ in chains)
        carry = lax.fori_loop(
            0, n_kb - n_full, lambda jj, c: block(n_kb - 1 - jj, c, True), init)

        def alive(c):
            return functools.reduce(jnp.maximum, [jnp.max(t0) for t0, _ in c]) > SB_DEAD_TAIL

        def cond(state):
            j, live, _ = state
            return (j >= 0) & live

        def body(state):
            j, _, c = state
            c = block(j, c, False)
            return j - 1, alive(c), c

        _, _, carry = lax.while_loop(cond, body, (n_full - 1, alive(carry), carry))
        for p, sl in enumerate(pair_lanes):
            o_ref[pl.ds(q_start, tq), sl] = _pair_select(
                carry[2 * p][1], carry[2 * p + 1][1]).astype(o_ref.dtype)
        return 0

    lax.fori_loop(0, sq // tq, tile, 0)


SB_PAIRS_PER_STEP = 2


def _sb_attn(q, k, v, *, tq, tk, q_off):
    nb, sq, w = q.shape
    sk = k.shape[1]
    pw = SB_PAIRS_PER_STEP * 2 * HEAD_DIM
    assert sk % tk == 0 and sq % tq == 0 and q_off + sq <= sk and w % pw == 0
    kern = functools.partial(_sb_kernel, tq=tq, tk=tk, q_off=q_off)
    return pl.pallas_call(
        kern,
        grid=(nb, w // pw),
        in_specs=[pl.BlockSpec((None, sq, pw), lambda b, p: (b, 0, p)),
                  pl.BlockSpec((None, sk, pw), lambda b, p: (b, 0, p)),
                  pl.BlockSpec((None, sk, pw), lambda b, p: (b, 0, p))],
        out_specs=pl.BlockSpec((None, sq, pw), lambda b, p: (b, 0, p)),
        out_shape=jax.ShapeDtypeStruct((nb, sq, w), BF16),
        compiler_params=_params(2),
        name="sb_attn",
    )(q, k, v)


HC = H_C * HEAD_DIM


def _odd_proj_kernel(x_ref, w_ref, q_ref, k32_ref, v32_ref, k16_ref, v16_ref):
    xb = x_ref[...].astype(BF16)
    h = _dot(xb, w_ref[...])
    q_ref[...] = (h[:, 0:HC] * QK_SCALE).astype(BF16)
    k = h[:, HC:2 * HC]
    v = h[:, 2 * HC:3 * HC]
    k32_ref[...] = k
    v32_ref[...] = v
    k16_ref[...] = k.astype(BF16)
    v16_ref[...] = v.astype(BF16)


def _odd_proj(x, w):
    t, d = x.shape
    tm = min(ROW_TILE, t)
    return pl.pallas_call(
        _odd_proj_kernel,
        grid=(t // tm,),
        in_specs=[_rows(tm, d), _resident(w.shape)],
        out_specs=(_rows(tm, HC),) * 5,
        out_shape=(jax.ShapeDtypeStruct((t, HC), BF16), jax.ShapeDtypeStruct((t, HC), F32),
                   jax.ShapeDtypeStruct((t, HC), F32), jax.ShapeDtypeStruct((t, HC), BF16),
                   jax.ShapeDtypeStruct((t, HC), BF16)),
        compiler_params=_params(1, V7X_VMEM_LIMIT_BYTES),
        name="odd_in_proj",
    )(x, w)


def _out_ln_kernel(*refs, n_mix, alpha):
    x_ref = refs[0]
    mix_refs = refs[1:1 + n_mix]
    w_refs = refs[1 + n_mix:1 + 2 * n_mix]
    g_ref, b_ref, o_ref = refs[1 + 2 * n_mix:]
    y = _dot(mix_refs[0][...], w_refs[0][...])
    for m_ref, w_ref in zip(mix_refs[1:], w_refs[1:]):
        y = y + _dot(m_ref[...], w_ref[...])
    o_ref[...] = _layer_norm(alpha * x_ref[...] + y, g_ref[...], b_ref[...])


def _out_ln(x, mixes, ws, g, b, alpha):
    t, d = x.shape
    tm = min(ROW_TILE, t)
    kern = functools.partial(_out_ln_kernel, n_mix=len(mixes), alpha=alpha)
    return pl.pallas_call(
        kern,
        grid=(t // tm,),
        in_specs=([_rows(tm, d)] + [_rows(tm, m.shape[1]) for m in mixes]
                  + [_resident(w.shape) for w in ws] + [_resident(g.shape), _resident(b.shape)]),
        out_specs=_rows(tm, d),
        out_shape=jax.ShapeDtypeStruct((t, d), F32),
        compiler_params=_params(1),
        name="out_proj_ln",
    )(x, *mixes, *ws, g, b)


def _ffn_ln_kernel(x_ref, wu_ref, wd_ref, g_ref, b_ref, o_ref, *, alpha):
    x = x_ref[...]
    h = _dot(x.astype(BF16), wu_ref[...])
    h = jnp.maximum(h, 0.0)
    ff = _dot((h * h).astype(BF16), wd_ref[...])
    o_ref[...] = _layer_norm(alpha * x + ff, g_ref[...], b_ref[...])


def _ffn_ln(x, wu, wd, g, b, alpha):
    t, d = x.shape
    tm = min(ROW_TILE, t)
    kern = functools.partial(_ffn_ln_kernel, alpha=alpha)
    return pl.pallas_call(
        kern,
        grid=(t // tm,),
        in_specs=[_rows(tm, d), _resident(wu.shape), _resident(wd.shape),
                  _resident(g.shape), _resident(b.shape)],
        out_specs=_rows(tm, d),
        out_shape=jax.ShapeDtypeStruct((t, d), F32),
        compiler_params=_params(1, V7X_VMEM_LIMIT_BYTES),
        name="ffn_ln",
    )(x, wu, wd, g, b)


def _rope_tables(pos, reps):
    inv = jnp.power(ROPE_THETA, -jnp.arange(ROPE_HALF, dtype=F32) / ROPE_HALF)
    ang = pos.astype(F32)[:, None] * inv[None, :]
    cos, sin = jnp.cos(ang), jnp.sin(ang)
    n = pos.shape[0]
    z = lambda w: jnp.zeros((n, w), F32)
    tail = LANES - ROPE_LANE0 - ROPE_DIM
    c = jnp.concatenate([jnp.ones((n, ROPE_LANE0), F32), cos, cos, z(tail)], axis=1)
    s1 = jnp.concatenate([z(ROPE_LANE0), -sin, z(ROPE_HALF), z(tail)], axis=1)
    s2 = jnp.concatenate([z(ROPE_LANE0), z(ROPE_HALF), sin, z(tail)], axis=1)
    return tuple(jnp.tile(a, (reps, 1)) for a in (c, s1, s2))


CAST_ROWS = 256


def _cast_kernel(w_ref, o_ref):
    o_ref[...] = w_ref[...].astype(o_ref.dtype)


def _to_bf16(w):
    r, c = w.shape
    return pl.pallas_call(
        _cast_kernel,
        grid=(r // CAST_ROWS,),
        in_specs=[_rows(CAST_ROWS, c)],
        out_specs=_rows(CAST_ROWS, c),
        out_shape=jax.ShapeDtypeStruct((r, c), BF16),
        compiler_params=_params(1),
        name="weight_cast",
    )(w)


def _even_weights(w_in, g_q, w_uq, g_kv, w_ukv, w_out):
    d = w_in.shape[0]
    o_r = Q_LORA + KV_LORA
    tail = LANES - ROPE_LANE0 - ROPE_DIM
    w_in = _to_bf16(w_in)
    w1 = jnp.concatenate(
        [w_in[:, :o_r], jnp.zeros((d, ROPE_LANE0), BF16), w_in[:, o_r:o_r + ROPE_DIM],
         jnp.zeros((d, tail), BF16), w_in[:, o_r + ROPE_DIM:]], axis=1)
    per_head = NOPE_DIM + ROPE_DIM
    wuq = jnp.pad(_to_bf16(w_uq).reshape(Q_LORA, H_A, per_head),
                  ((0, 0), (0, 0), (0, MLA_HEAD_LANES - per_head)))
    wuq = wuq.reshape(Q_LORA, H_A * MLA_HEAD_LANES)
    wkv = _to_bf16(w_ukv).reshape(KV_LORA, H_A, NOPE_DIM + V_DIM)
    wk = jnp.pad(wkv[:, :, :NOPE_DIM], ((0, 0), (0, 0), (0, MLA_HEAD_LANES - NOPE_DIM)))
    wk = wk.reshape(KV_LORA, H_A * MLA_HEAD_LANES)
    wv = wkv[:, :, NOPE_DIM:].reshape(KV_LORA, H_A * V_DIM)
    wo = _to_bf16(w_out)
    wo_a, wo_b = wo[:H_A * V_DIM], wo[H_A * V_DIM:]
    return dict(w1=w1, gq=g_q[None, :], gkv=g_kv[None, :], wuq=wuq, wk=wk, wv=wv,
                wo_a=wo_a, wo_b=wo_b)


def _round_up(n, m):
    return (n + m - 1) // m * m


def _pad_rows(a, rows):
    return jnp.pad(a, ((0, 0), (0, rows - a.shape[1]), (0, 0)))


ATTN_TILE = 256


def _even_mixer(x, nb, t, ropes, w, table, cache):
    qpad, ckv, krw, qb, kb32, vb32, kb16, vb16 = _even_proj(x, w["w1"], w["gq"], w["gkv"], w["wuq"], ropes)
    kpad, v16 = _kv_up(ckv, krw, w["wk"], w["wv"])
    r3 = lambda a: a.reshape(nb, t, a.shape[-1])
    qpad, kpad, v16, qb, kb16, vb16 = map(r3, (qpad, kpad, v16, qb, kb16, vb16))
    kb32 = kb32.reshape(nb, t, H_B, HEAD_DIM)
    vb32 = vb32.reshape(nb, t, H_B, HEAD_DIM)
    if cache is None:
        tq = min(ATTN_TILE, t)
        o_a = _mla_attn(qpad, kpad, v16, tq=tq, tk=min(2 * tq, t), causal=True, kv_len=t)
        nkb = min(BAND_PAST // tq + 1, t // tq)
        bias = _band_bias(table, tq=tq, tk=tq, nkb=nkb, base_off=(nkb - 1) * tq,
                          band_mask=True, kv_len=t)
        o_b = _band_attn(qb, kb16, vb16, bias, tq=tq, tk=tq, nkb=nkb)
        rows = min(BAND_PAST, t)
        new_bk, new_bv = kb32[:, t - rows:], vb32[:, t - rows:]
    else:
        c_ckv, c_kr, c_bk, c_bv = cache
        past = c_ckv.shape[1]
        tail = LANES - ROPE_LANE0 - ROPE_DIM
        c_krw = jnp.pad(c_kr, ((0, 0), (0, 0), (ROPE_LANE0, tail)))
        kpad_c, v16_c = _kv_up(c_ckv.reshape(nb * past, KV_LORA), c_krw.reshape(nb * past, LANES),
                               w["wk"], w["wv"])
        kv_len = past + t
        sk = _round_up(kv_len, LANES)
        k_all = _pad_rows(jnp.concatenate([kpad_c.reshape(nb, past, -1), kpad], axis=1), sk)
        v_all = _pad_rows(jnp.concatenate([v16_c.reshape(nb, past, -1), v16], axis=1), sk)
        o_a = _mla_attn(qpad, k_all, v_all, tq=t, tk=sk, causal=False, kv_len=kv_len)
        n_past = c_bk.shape[1]
        bk_all = jnp.concatenate([c_bk, kb32], axis=1)
        bv_all = jnp.concatenate([c_bv, vb32], axis=1)
        band_len = n_past + t
        bsk = _round_up(band_len, LANES)
        flat16 = lambda a: _pad_rows(a.reshape(nb, band_len, HB).astype(BF16), bsk)
        bias = _band_bias(table, tq=t, tk=bsk, nkb=1, base_off=n_past, band_mask=False,
                          kv_len=band_len)
        o_b = _band_attn(qb, flat16(bk_all), flat16(bv_all), bias, tq=t, tk=bsk, nkb=1)
        rows = min(BAND_PAST, band_len)
        new_bk, new_bv = bk_all[:, band_len - rows:], bv_all[:, band_len - rows:]
    mixes = (o_a.reshape(nb * t, -1), o_b.reshape(nb * t, -1))
    new_ckv = ckv.reshape(nb, t, KV_LORA)
    new_kr = krw[:, ROPE_LANE0:ROPE_LANE0 + ROPE_DIM].reshape(nb, t, ROPE_DIM)
    return mixes, (w["wo_a"], w["wo_b"]), (new_ckv, new_kr, new_bk, new_bv)


def _odd_mixer(x, nb, t, w_in, w_out, cache):
    q16, k32, v32, k16, v16 = _odd_proj(x, w_in)
    r3 = lambda a: a.reshape(nb, t, HC)
    q16, k16, v16 = map(r3, (q16, k16, v16))
    if cache is None:
        tq = min(ATTN_TILE, t)
        o = _sb_attn(q16, k16, v16, tq=tq, tk=tq, q_off=0)
    else:
        c_k, c_v = cache
        past = c_k.shape[1]
        tk = 3 * LANES
        sk = _round_up(past + t, tk)
        cat = lambda c, n: _pad_rows(
            jnp.concatenate([c.reshape(nb, past, HC).astype(BF16), n], axis=1), sk)
        o = _sb_attn(q16, cat(c_k, k16), cat(c_v, v16), tq=t, tk=tk, q_off=past)
    new_k = k32.reshape(nb, t, H_C, HEAD_DIM)
    new_v = v32.reshape(nb, t, H_C, HEAD_DIM)
    return (o.reshape(nb * t, HC),), (w_out,), (new_k, new_v)


def kernel(x_prompt, x_sample, cache_mla_ckv, cache_mla_krope, cache_band_k, cache_band_v,
           cache_sb_k, cache_sb_v, w_in_ab, g_q_lat, w_uq, g_kv_lat, w_ukv, rel_bias, w_out_ab,
           w_in_c, w_out_c, ln_mix_g, ln_mix_b, ln_ffn_g, ln_ffn_b, w_ff_up, w_ff_down):
    nb_p, t_p, d = x_prompt.shape
    nb_s, t_s, _ = x_sample.shape
    past = cache_mla_ckv.shape[2]
    depth = ln_mix_g.shape[0]
    alpha = (2.0 * depth) ** 0.25
    xp = x_prompt.reshape(nb_p * t_p, d)
    xs = x_sample.reshape(nb_s * t_s, d)
    ropes_p = _rope_tables(jnp.arange(t_p, dtype=jnp.int32), nb_p)
    ropes_s = _rope_tables(past + jnp.arange(t_s, dtype=jnp.int32), nb_s)
    even_p, even_s, odd_p, odd_s = [], [], [], []
    for l in range(depth):
        i = l // 2
        if l % 2 == 0:
            w = _even_weights(w_in_ab[i], g_q_lat[i], w_uq[i], g_kv_lat[i], w_ukv[i], w_out_ab[i])
            mp, wo, new_p = _even_mixer(xp, nb_p, t_p, ropes_p, w, rel_bias[i], None)
            ms, _, new_s = _even_mixer(
                xs, nb_s, t_s, ropes_s, w, rel_bias[i],
                (cache_mla_ckv[i], cache_mla_krope[i], cache_band_k[i], cache_band_v[i]))
            even_p.append(new_p)
            even_s.append(new_s)
        else:
            w_in = _to_bf16(w_in_c[i])
            w_out = _to_bf16(w_out_c[i])
            mp, wo, new_p = _odd_mixer(xp, nb_p, t_p, w_in, w_out, None)
            ms, _, new_s = _odd_mixer(xs, nb_s, t_s, w_in, w_out, (cache_sb_k[i], cache_sb_v[i]))
            odd_p.append(new_p)
            odd_s.append(new_s)
        g1, b1 = ln_mix_g[l][None, :], ln_mix_b[l][None, :]
        g2, b2 = ln_ffn_g[l][None, :], ln_ffn_b[l][None, :]
        wu, wd = _to_bf16(w_ff_up[l]), _to_bf16(w_ff_down[l])
        xp = _ffn_ln(_out_ln(xp, mp, wo, g1, b1, alpha), wu, wd, g2, b2, alpha)
        xs = _ffn_ln(_out_ln(xs, ms, wo, g1, b1, alpha), wu, wd, g2, b2, alpha)
    stack = lambda groups, k: jnp.stack([g[k] for g in groups])
    return (xp.reshape(nb_p, t_p, d), xs.reshape(nb_s, t_s, d),
            stack(even_p, 0), stack(even_p, 1), stack(even_p, 2), stack(even_p, 3),
            stack(odd_p, 0), stack(odd_p, 1),
            stack(even_s, 0), stack(even_s, 1), stack(even_s, 2), stack(even_s, 3),
            stack(odd_s, 0), stack(odd_s, 1))
```

```python
import functools

import jax
import jax.numpy as jnp
from jax import lax
from jax.experimental import pallas as pl
from jax.experimental.pallas import tpu as pltpu

F32 = jnp.float32
BF16 = jnp.bfloat16

CHUNK = 64
CHUNK_SHIFT = 6
HEAD_DIM = 64
H_A = 8
Q_LORA = 768
KV_LORA = 256
NOPE_DIM = 64
ROPE_DIM = 32
V_DIM = 64
ROPE_THETA = 10000.0
MLA_SCALE = (NOPE_DIM + ROPE_DIM) ** -0.5
H_B = 8
PREV_CHUNKS = 8
BAND_PAST = PREV_CHUNKS * CHUNK
REL_MAX = 256
N_REL = REL_MAX + CHUNK
H_C = 16
QK_SCALE = HEAD_DIM ** -0.5
NEG_INF = -1e30

LANES = 128
SUBLANES = 8
V7X_VMEM_LIMIT_BYTES = 56 * 1024 * 1024

MLA_HEAD_LANES = LANES
ROPE_LANE0 = NOPE_DIM
ROPE_HALF = ROPE_DIM // 2

ROW_TILE = 512


def _params(n_axes, vmem_bytes=None):
    return pltpu.CompilerParams(
        dimension_semantics=("parallel",) * n_axes,
        vmem_limit_bytes=vmem_bytes,
    )


def _resident(shape):
    nd = len(shape)
    return pl.BlockSpec(shape, lambda *_: (0,) * nd, pipeline_mode=pl.Buffered(1))


def _rows(tm, width):
    return pl.BlockSpec((tm, width), lambda i: (i, 0))


def _dot(a, b):
    return jnp.dot(a, b, preferred_element_type=F32)


def _dot_nt(a, b):
    return lax.dot_general(a, b, (((1,), (1,)), ((), ())), preferred_element_type=F32)


def _rms(h, g, eps=1e-6):
    return h * lax.rsqrt(jnp.mean(h * h, axis=-1, keepdims=True) + eps) * g


def _layer_norm(r, g, b, eps=1e-5):
    mu = jnp.mean(r, axis=-1, keepdims=True)
    d = r - mu
    var = jnp.mean(d * d, axis=-1, keepdims=True)
    return d * lax.rsqrt(var + eps) * g + b


N_BAND = 3 * H_B * HEAD_DIM
W1_Q0, W1_Q1 = 0, Q_LORA
W1_C0, W1_C1 = W1_Q1, W1_Q1 + KV_LORA
W1_R0, W1_R1 = W1_C1, W1_C1 + LANES
W1_B0, W1_B1 = W1_R1, W1_R1 + N_BAND
HB = H_B * HEAD_DIM


def _even_proj_kernel(x_ref, w1_ref, gq_ref, gkv_ref, wuq_ref, c_ref, s1_ref, s2_ref,
                      qpad_ref, ckv_ref, krw_ref, qb_ref, kb32_ref, vb32_ref, kb16_ref, vb16_ref):
    xb = x_ref[...].astype(BF16)
    c = c_ref[...]
    s1 = s1_ref[...]
    s2 = s2_ref[...]

    def rope(v):
        return (v * c + pltpu.roll(v, LANES - ROPE_HALF, 1) * s1
                + pltpu.roll(v, ROPE_HALF, 1) * s2)

    hq = _dot(xb, w1_ref[:, W1_Q0:W1_Q1])
    qn = _rms(hq, gq_ref[...]).astype(BF16)
    qa = _dot(qn, wuq_ref[...])
    for h in range(H_A):
        sl = slice(h * MLA_HEAD_LANES, (h + 1) * MLA_HEAD_LANES)
        qpad_ref[:, sl] = rope(qa[:, sl]).astype(BF16)

    hc = _dot(xb, w1_ref[:, W1_C0:W1_C1])
    ckv_ref[...] = _rms(hc, gkv_ref[...])

    hk = _dot(xb, w1_ref[:, W1_R0:W1_R1])
    krw_ref[...] = rope(hk)

    hb = _dot(xb, w1_ref[:, W1_B0:W1_B1])
    qb_ref[...] = (hb[:, 0:HB] * QK_SCALE).astype(BF16)
    kb = hb[:, HB:2 * HB]
    vb = hb[:, 2 * HB:3 * HB]
    kb32_ref[...] = kb
    vb32_ref[...] = vb
    kb16_ref[...] = kb.astype(BF16)
    vb16_ref[...] = vb.astype(BF16)


def _even_proj(x, w1, gq, gkv, wuq, rope_tabs):
    t, d = x.shape
    tm = min(ROW_TILE, t)
    c, s1, s2 = rope_tabs
    qw = H_A * MLA_HEAD_LANES
    out_shape = (
        jax.ShapeDtypeStruct((t, qw), BF16),
        jax.ShapeDtypeStruct((t, KV_LORA), F32),
        jax.ShapeDtypeStruct((t, LANES), F32),
        jax.ShapeDtypeStruct((t, HB), BF16),
        jax.ShapeDtypeStruct((t, HB), F32),
        jax.ShapeDtypeStruct((t, HB), F32),
        jax.ShapeDtypeStruct((t, HB), BF16),
        jax.ShapeDtypeStruct((t, HB), BF16),
    )
    return pl.pallas_call(
        _even_proj_kernel,
        grid=(t // tm,),
        in_specs=[_rows(tm, d), _resident(w1.shape), _resident(gq.shape), _resident(gkv.shape),
                  _resident(wuq.shape), _rows(tm, LANES), _rows(tm, LANES), _rows(tm, LANES)],
        out_specs=(_rows(tm, qw), _rows(tm, KV_LORA), _rows(tm, LANES), _rows(tm, HB),
                   _rows(tm, HB), _rows(tm, HB), _rows(tm, HB), _rows(tm, HB)),
        out_shape=out_shape,
        compiler_params=_params(1, V7X_VMEM_LIMIT_BYTES),
        name="even_in_proj",
    )(x, w1, gq, gkv, wuq, c, s1, s2)


def _kv_up_kernel(ckv_ref, krw_ref, wk_ref, wv_ref, kpad_ref, v_ref, *, values_on_rows):
    cb = ckv_ref[...].astype(BF16)
    krw = krw_ref[...]
    k = _dot(cb, wk_ref[...])
    for h in range(H_A):
        sl = slice(h * MLA_HEAD_LANES, (h + 1) * MLA_HEAD_LANES)
        kpad_ref[:, sl] = (k[:, sl] + krw).astype(BF16)
    if values_on_rows:
        v_ref[...] = _dot_nt(wv_ref[...], cb).astype(BF16)
    else:
        v_ref[...] = _dot(cb, wv_ref[...]).astype(BF16)


def _kv_up(ckv, krw, wk, wv):
    t = ckv.shape[0]
    tm = min(ROW_TILE, t)
    kw = H_A * MLA_HEAD_LANES
    vw = H_A * V_DIM
    return pl.pallas_call(
        functools.partial(_kv_up_kernel, values_on_rows=False),
        grid=(t // tm,),
        in_specs=[_rows(tm, KV_LORA), _rows(tm, LANES), _resident(wk.shape), _resident(wv.shape)],
        out_specs=(_rows(tm, kw), _rows(tm, vw)),
        out_shape=(jax.ShapeDtypeStruct((t, kw), BF16), jax.ShapeDtypeStruct((t, vw), BF16)),
        compiler_params=_params(1),
        name="mla_kv_up",
    )(ckv, krw, wk, wv)


def _kv_up_blocked(ckv, krw, wk, wv_t, nb, seq, tk):
    t = ckv.shape[0]
    kw = H_A * MLA_HEAD_LANES
    vw = H_A * V_DIM
    per_seq = seq // tk
    return pl.pallas_call(
        functools.partial(_kv_up_kernel, values_on_rows=True),
        grid=(t // tk,),
        in_specs=[_rows(tk, KV_LORA), _rows(tk, LANES), _resident(wk.shape), _resident(wv_t.shape)],
        out_specs=(_rows(tk, kw),
                   pl.BlockSpec((None, None, vw, tk), lambda i: (i // per_seq, i % per_seq, 0, 0))),
        out_shape=(jax.ShapeDtypeStruct((t, kw), BF16),
                   jax.ShapeDtypeStruct((nb, per_seq, vw, tk), BF16)),
        compiler_params=_params(1),
        name="mla_kv_up_blocked",
    )(ckv, krw, wk, wv_t)


def _pair_select(lo, hi):
    lane = lax.broadcasted_iota(jnp.int32, lo.shape, 1)
    return jnp.where(lane < HEAD_DIM, lo, hi)


def _mla_sample_kernel(q_ref, kc_ref, vc_ref, kn_ref, vn_ref, o_ref):
    n_heads = q_ref.shape[1] // MLA_HEAD_LANES
    hsls = [slice(h * MLA_HEAD_LANES, (h + 1) * MLA_HEAD_LANES) for h in range(n_heads)]
    vsls = [slice(h // 2 * 2 * V_DIM, (h // 2 + 1) * 2 * V_DIM) for h in range(n_heads)]
    segments = ((kc_ref, vc_ref), (kn_ref, vn_ref))
    exp_scale = MLA_SCALE * LOG2_E
    raw = [[_dot_nt(q_ref[:, hsl], k_ref[:, hsl]) for k_ref, _ in segments] for hsl in hsls]
    probs, sums = [], []
    for scores in raw:
        m = functools.reduce(jnp.maximum, [s.max(axis=-1, keepdims=True) for s in scores])
        ps = [jnp.exp2((s - m) * exp_scale) for s in scores]
        sums.append(functools.reduce(jnp.add, [p.sum(axis=-1, keepdims=True) for p in ps]))
        probs.append([p.astype(BF16) for p in ps])
    outs = []
    for vsl, ps, l in zip(vsls, probs, sums):
        acc = functools.reduce(
            jnp.add, [_dot(p, v_ref[:, vsl]) for p, (_, v_ref) in zip(ps, segments)])
        outs.append(acc / l)
    for p in range(n_heads // 2):
        o_ref[:, vsls[2 * p]] = _pair_select(outs[2 * p], outs[2 * p + 1]).astype(o_ref.dtype)


MLA_PAIRS_PER_STEP = 2


def _mla_sample_attn(q, k_cache, v_cache, k_new, v_new):
    nb, t, _ = q.shape
    past = k_cache.shape[1]
    qk_w = MLA_PAIRS_PER_STEP * 2 * MLA_HEAD_LANES
    v_w = MLA_PAIRS_PER_STEP * 2 * V_DIM
    spec = lambda rows, width: pl.BlockSpec((None, rows, width), lambda b, p: (b, 0, p))
    return pl.pallas_call(
        _mla_sample_kernel,
        grid=(nb, H_A * V_DIM // v_w),
        in_specs=[spec(t, qk_w), spec(past, qk_w), spec(past, v_w), spec(t, qk_w), spec(t, v_w)],
        out_specs=spec(t, v_w),
        out_shape=jax.ShapeDtypeStruct((nb, t, H_A * V_DIM), BF16),
        compiler_params=_params(2),
        name="mla_sample_attn",
    )(q, k_cache, v_cache, k_new, v_new)


def _mla_prompt_kernel(q_ref, k_ref, vt_ref, o_ref, *, tq, tk):
    i = pl.program_id(2)
    n_heads = q_ref.shape[1] // MLA_HEAD_LANES
    n_full = (i * tq) // tk
    n_kb = ((i + 1) * tq + tk - 1) // tk
    k_row = lax.broadcasted_iota(jnp.int32, (tk, tq), 0)
    q_col = lax.broadcasted_iota(jnp.int32, (tk, tq), 1)
    q_chunk = (q_col + i * tq) >> CHUNK_SHIFT
    hsls = [slice(h * MLA_HEAD_LANES, (h + 1) * MLA_HEAD_LANES) for h in range(n_heads)]
    vsls = [slice(h // 2 * 2 * V_DIM, (h // 2 + 1) * 2 * V_DIM) for h in range(n_heads)]
    qs = [q_ref[:, hsl] for hsl in hsls]
    exp_scale = MLA_SCALE * LOG2_E

    def block(j, carry, masked):
        start = pl.multiple_of(j * tk, tk)
        ss = [_dot_nt(k_ref[pl.ds(start, tk), hsl], q) for q, hsl in zip(qs, hsls)]
        ps, stats = [], []
        for s, (m, l, _) in zip(ss, carry):
            if masked:
                s = jnp.where(((k_row + j * tk) >> CHUNK_SHIFT) <= q_chunk, s, NEG_INF)
            m_new = jnp.maximum(m, jnp.max(s, axis=0, keepdims=True))
            alpha = jnp.exp2((m - m_new) * exp_scale)
            p = jnp.exp2((s - m_new) * exp_scale)
            ps.append(p.astype(BF16))
            stats.append((m_new, alpha, alpha * l + jnp.sum(p, axis=0, keepdims=True)))
        return tuple(
            (m_new, l, alpha * acc + _dot(vt_ref[j, vsl, :], p))
            for p, vsl, (m_new, alpha, l), (_, _, acc) in zip(ps, vsls, stats, carry))

    init = tuple((jnp.full((1, tq), NEG_INF, F32), jnp.zeros((1, tq), F32),
                  jnp.zeros((2 * V_DIM, tq), F32)) for _ in range(n_heads))
    carry = lax.fori_loop(0, n_full, functools.partial(block, masked=False), init)
    carry = lax.fori_loop(n_full, n_kb, functools.partial(block, masked=True), carry)
    outs = [acc / l for _, l, acc in carry]
    v_row = lax.broadcasted_iota(jnp.int32, (2 * V_DIM, tq), 0)
    for p in range(n_heads // 2):
        pair = jnp.where(v_row < V_DIM, outs[2 * p], outs[2 * p + 1])
        o_ref[:, vsls[2 * p]] = pair.T.astype(o_ref.dtype)


def _mla_prompt_attn(q, k, vt, *, tq, tk):
    nb, sq, _ = q.shape
    pairs = H_A // (2 * MLA_PAIRS_PER_STEP)
    qk_w = MLA_PAIRS_PER_STEP * 2 * MLA_HEAD_LANES
    v_w = MLA_PAIRS_PER_STEP * 2 * V_DIM
    assert vt.shape == (nb, sq // tk, H_A * V_DIM, tk)
    kern = functools.partial(_mla_prompt_kernel, tq=tq, tk=tk)
    return pl.pallas_call(
        kern,
        grid=(nb, pairs, sq // tq),
        in_specs=[pl.BlockSpec((None, tq, qk_w), lambda b, p, i: (b, i, p)),
                  pl.BlockSpec((None, sq, qk_w), lambda b, p, i: (b, 0, p)),
                  pl.BlockSpec((None, sq // tk, v_w, tk), lambda b, p, i: (b, 0, p, 0))],
        out_specs=pl.BlockSpec((None, tq, v_w), lambda b, p, i: (b, i, p)),
        out_shape=jax.ShapeDtypeStruct((nb, sq, H_A * V_DIM), BF16),
        compiler_params=_params(3),
        name="mla_prompt_attn",
    )(q, k, vt)


def _half_masks(x):
    lane = lax.broadcasted_iota(jnp.int32, x.shape, 1)
    zero = jnp.zeros_like(x)
    return jnp.where(lane < HEAD_DIM, x, zero), jnp.where(lane < HEAD_DIM, zero, x)


def _band_core(q_ref, o_ref, segments):
    n_pairs = q_ref.shape[1] // (2 * HEAD_DIM)
    pair_lanes = [slice(p * 2 * HEAD_DIM, (p + 1) * 2 * HEAD_DIM) for p in range(n_pairs)]
    chains = [(2 * p + hh, q, sl) for p, sl in enumerate(pair_lanes)
              for hh, q in enumerate(_half_masks(q_ref[:, sl]))]
    raw = [[_dot_nt(q, keys(sl)) for keys, _, _, _ in segments] for _, q, sl in chains]
    probs, sums = [], []
    for (h, _, _), head_raw in zip(chains, raw):
        scores = []
        for s, (_, _, bias, live) in zip(head_raw, segments):
            s = s + bias(h)
            scores.append(s if live is None else jnp.where(live, s, NEG_INF))
        m = functools.reduce(jnp.maximum, [s.max(axis=-1, keepdims=True) for s in scores])
        ps = [jnp.exp(s - m) for s in scores]
        sums.append(functools.reduce(jnp.add, [p.sum(axis=-1, keepdims=True) for p in ps]))
        probs.append([p.astype(BF16) for p in ps])
    outs = []
    for (_, _, sl), ps, l in zip(chains, probs, sums):
        acc = functools.reduce(
            jnp.add, [_dot(p, values(sl)) for p, (_, values, _, _) in zip(ps, segments)])
        outs.append(acc / l)
    for p, sl in enumerate(pair_lanes):
        o_ref[:, sl] = _pair_select(outs[2 * p], outs[2 * p + 1]).astype(o_ref.dtype)


def _band_kernel(q_ref, k_ref, v_ref, bias_ref, o_ref, *, tq, tk, nkb):
    i = pl.program_id(2)
    segments = []
    for d in range(nkb):
        kbi = i - (nkb - 1) + d
        start = pl.multiple_of(jnp.maximum(kbi, 0) * tk, tk)
        segments.append((lambda sl, start=start: k_ref[pl.ds(start, tk), sl],
                         lambda sl, start=start: v_ref[pl.ds(start, tk), sl],
                         lambda h, d=d: bias_ref[h, d], kbi >= 0))
    _band_core(q_ref, o_ref, segments)


def _band_sample_kernel(q_ref, kc_ref, vc_ref, kn_ref, vn_ref, bc_ref, bn_ref, o_ref):
    _band_core(q_ref, o_ref, (
        (lambda sl: kc_ref[:, sl], lambda sl: vc_ref[:, sl], lambda h: bc_ref[h], None),
        (lambda sl: kn_ref[:, sl], lambda sl: vn_ref[:, sl], lambda h: bn_ref[h], None)))


BAND_PAIRS_PER_STEP = 2


def _band_attn(q, k, v, bias, *, tq, tk, nkb):
    nb, sq, w = q.shape
    sk = k.shape[1]
    pw = BAND_PAIRS_PER_STEP * 2 * HEAD_DIM
    heads_per_step = 2 * BAND_PAIRS_PER_STEP
    assert bias.shape == (H_B, nkb, tq, tk) and w % pw == 0
    assert tq == tk or (nkb == 1 and sq == tq and sk == tk)
    kern = functools.partial(_band_kernel, tq=tq, tk=tk, nkb=nkb)
    return pl.pallas_call(
        kern,
        grid=(nb, w // pw, sq // tq),
        in_specs=[pl.BlockSpec((None, tq, pw), lambda b, p, i: (b, i, p)),
                  pl.BlockSpec((None, sk, pw), lambda b, p, i: (b, 0, p)),
                  pl.BlockSpec((None, sk, pw), lambda b, p, i: (b, 0, p)),
                  pl.BlockSpec((heads_per_step, nkb, tq, tk), lambda b, p, i: (p, 0, 0, 0))],
        out_specs=pl.BlockSpec((None, tq, pw), lambda b, p, i: (b, i, p)),
        out_shape=jax.ShapeDtypeStruct((nb, sq, w), BF16),
        compiler_params=_params(3),
        name="band_attn",
    )(q, k, v, bias)


def _band_sample_attn(q, k_cache, v_cache, k_new, v_new, bias_cache, bias_new):
    nb, t, w = q.shape
    n_past = k_cache.shape[1]
    pw = BAND_PAIRS_PER_STEP * 2 * HEAD_DIM
    heads_per_step = 2 * BAND_PAIRS_PER_STEP
    assert bias_cache.shape == (H_B, t, n_past) and bias_new.shape == (H_B, t, t)
    spec = lambda rows: pl.BlockSpec((None, rows, pw), lambda b, p: (b, 0, p))
    bias_spec = lambda cols: pl.BlockSpec((heads_per_step, t, cols), lambda b, p: (p, 0, 0))
    return pl.pallas_call(
        _band_sample_kernel,
        grid=(nb, w // pw),
        in_specs=[spec(t), spec(n_past), spec(n_past), spec(t), spec(t),
                  bias_spec(n_past), bias_spec(t)],
        out_specs=spec(t),
        out_shape=jax.ShapeDtypeStruct((nb, t, w), BF16),
        compiler_params=_params(2),
        name="band_sample_attn",
    )(q, k_cache, v_cache, k_new, v_new, bias_cache, bias_new)


def _band_bias_kernel(tab_ref, o_ref, *, tq, tk, nkb, base_off, band_mask, kv_len):
    h = pl.program_id(0)
    d = pl.program_id(1)
    row = lax.broadcasted_iota(jnp.int32, (tq, tk), 0)
    col = lax.broadcasted_iota(jnp.int32, (tq, tk), 1)
    width = _round_up(tq + tk - 1, LANES)
    u = lax.broadcasted_iota(jnp.int32, (SUBLANES, width), 1)
    col_minus_row = jnp.where(u < tk, u, u - width)
    dist = (base_off - d * tk) - col_minus_row
    idx = jnp.clip(dist, -(CHUNK - 1), REL_MAX) + (CHUNK - 1)

    def body(r, acc):
        return jnp.where(idx == r, tab_ref[h, r], acc)

    line = lax.fori_loop(0, N_REL, body, jnp.zeros((SUBLANES, width), F32))
    spread = jnp.broadcast_to(line[0:1, :], (tq, width))
    bias = pltpu.roll(spread, 0, 1, stride=1, stride_axis=0)[:, :tk]
    if band_mask:
        q_chunk = ((nkb - 1) * tk + row) >> CHUNK_SHIFT
        k_chunk = (d * tk + col) >> CHUNK_SHIFT
        gap = q_chunk - k_chunk
        valid = (gap >= 0) & (gap <= PREV_CHUNKS)
    else:
        valid = (d * tk + col) < kv_len
    o_ref[...] = jnp.where(valid, bias, NEG_INF)


def _band_bias(table, *, tq, tk, nkb, base_off, band_mask, kv_len):
    kern = functools.partial(_band_bias_kernel, tq=tq, tk=tk, nkb=nkb, base_off=base_off,
                             band_mask=band_mask, kv_len=kv_len)
    return pl.pallas_call(
        kern,
        grid=(H_B, nkb),
        in_specs=[pl.BlockSpec(memory_space=pltpu.SMEM)],
        out_specs=pl.BlockSpec((None, None, tq, tk), lambda h, d: (h, d, 0, 0)),
        out_shape=jax.ShapeDtypeStruct((H_B, nkb, tq, tk), F32),
        compiler_params=_params(2),
        name="band_bias",
    )(table)


SB_DEAD_TAIL = -110.0
LOG2_E = 1.4426950408889634


def _sb_later2(tk):
    ur = lax.broadcasted_iota(jnp.int32, (tk, tk), 0)
    uc = lax.broadcasted_iota(jnp.int32, (tk, tk), 1)
    later = jnp.where(ur > uc, -1.0, 0.0).astype(BF16)
    return jnp.concatenate([later, later], axis=0)


def _sb_init(tq, n_heads):
    return tuple((jnp.zeros((tq, 1), F32), jnp.zeros((tq, 2 * HEAD_DIM), F32))
                 for _ in range(n_heads))


def _sb_block(qs, kbs, vbs, carry, later2, valid):
    zs = [_dot_nt(q, kb) for q, kb in zip(qs, kbs)]
    sps, log_sigs = [], []
    for z in zs:
        t = jnp.log(1.0 + jnp.exp2(jnp.abs(z) * -LOG2_E))
        sp = jnp.maximum(z, 0.0) + t
        log_sigs.append(jnp.minimum(z, 0.0) - t)
        sps.append(sp if valid is None else jnp.where(valid, sp, 0.0))
    tails = []
    for sp, (tail0, _) in zip(sps, carry):
        sp_hi = sp.astype(BF16)
        sp_lo = (sp - sp_hi.astype(F32)).astype(BF16)
        tails.append(_dot(jnp.concatenate([sp_hi, sp_lo], axis=1), later2) + tail0)
    out = []
    for vb, sp, log_sig, tail, (tail0, acc) in zip(vbs, sps, log_sigs, tails, carry):
        w = jnp.exp(log_sig + tail)
        if valid is not None:
            w = jnp.where(valid, w, 0.0)
        acc = acc + _dot(w.astype(BF16), vb)
        tail0 = tail0 - jnp.sum(sp, axis=-1, keepdims=True)
        out.append((tail0, acc))
    return tuple(out)


def _sb_sweep(block, n_blocks, carry):
    def alive(c):
        return functools.reduce(jnp.maximum, [jnp.max(t0) for t0, _ in c]) > SB_DEAD_TAIL

    def cond(state):
        j, live, _ = state
        return (j >= 0) & live

    def body(state):
        j, _, c = state
        c = block(j, c)
        return j - 1, alive(c), c

    return lax.while_loop(cond, body, (n_blocks - 1, alive(carry), carry))[2]


def _sb_kernel(q_ref, k_ref, v_ref, o_ref, *, tq, tk, q_off):
    sq = q_ref.shape[0]
    n_pairs = q_ref.shape[1] // (2 * HEAD_DIM)
    pair_lanes = [slice(p * 2 * HEAD_DIM, (p + 1) * 2 * HEAD_DIM) for p in range(n_pairs)]
    row = lax.broadcasted_iota(jnp.int32, (tq, tk), 0)
    col = lax.broadcasted_iota(jnp.int32, (tq, tk), 1)
    later2 = _sb_later2(tk)

    def tile(i, _):
        q_start = pl.multiple_of(i * tq, tq)
        q0 = i * tq + q_off
        n_full = q0 // tk
        n_kb = (q0 + tq - 1 + tk - 1) // tk
        chains = [(q, sl) for sl in pair_lanes
                  for q in _half_masks(q_ref[pl.ds(q_start, tq), sl])]

        def block(j, carry, masked):
            start = pl.multiple_of(j * tk, tk)
            valid = (col + j * tk) < (row + q0) if masked else None
            return _sb_block([q for q, _ in chains],
                             [k_ref[pl.ds(start, tk), sl] for _, sl in chains],
                             [v_ref[pl.ds(start, tk), sl] for _, sl in chains],
                             carry, later2, valid)

        carry = lax.fori_loop(
            0, n_kb - n_full, lambda jj, c: block(n_kb - 1 - jj, c, True),
            _sb_init(tq, len(chains)))
        carry = _sb_sweep(lambda j, c: block(j, c, False), n_full, carry)
        for p, sl in enumerate(pair_lanes):
            o_ref[pl.ds(q_start, tq), sl] = _pair_select(
                carry[2 * p][1], carry[2 * p + 1][1]).astype(o_ref.dtype)
        return 0

    lax.fori_loop(0, sq // tq, tile, 0)


SB_PAIRS_PER_STEP = 2


def _sb_attn(q, k, v, *, tq, tk, q_off):
    nb, sq, w = q.shape
    sk = k.shape[1]
    pw = SB_PAIRS_PER_STEP * 2 * HEAD_DIM
    assert sk % tk == 0 and sq % tq == 0 and q_off + sq <= sk and w % pw == 0
    kern = functools.partial(_sb_kernel, tq=tq, tk=tk, q_off=q_off)
    return pl.pallas_call(
        kern,
        grid=(nb, w // pw),
        in_specs=[pl.BlockSpec((None, sq, pw), lambda b, p: (b, 0, p)),
                  pl.BlockSpec((None, sk, pw), lambda b, p: (b, 0, p)),
                  pl.BlockSpec((None, sk, pw), lambda b, p: (b, 0, p))],
        out_specs=pl.BlockSpec((None, sq, pw), lambda b, p: (b, 0, p)),
        out_shape=jax.ShapeDtypeStruct((nb, sq, w), BF16),
        compiler_params=_params(2),
        name="sb_attn",
    )(q, k, v)


SB_CACHE_BLOCK = 256


def _sb_sample_kernel(q_ref, kc_ref, vc_ref, kn_ref, vn_ref, o_ref, *, tc):
    t = q_ref.shape[0]
    past = kc_ref.shape[0]
    n_pairs = q_ref.shape[1] // (2 * HEAD_DIM)
    pair_lanes = [slice(p * 2 * HEAD_DIM, (p + 1) * 2 * HEAD_DIM) for p in range(n_pairs)]
    qs = [q for sl in pair_lanes for q in _half_masks(q_ref[:, sl])]
    per_head = lambda blocks: [b for b in blocks for _ in range(2)]
    row = lax.broadcasted_iota(jnp.int32, (t, t), 0)
    col = lax.broadcasted_iota(jnp.int32, (t, t), 1)
    carry = _sb_block(qs, per_head([kn_ref[:, sl] for sl in pair_lanes]),
                      per_head([vn_ref[:, sl] for sl in pair_lanes]),
                      _sb_init(t, len(qs)), _sb_later2(t), col < row)
    later2 = _sb_later2(tc)

    def block(j, c):
        start = pl.multiple_of(j * tc, tc)
        kbs = [kc_ref[pl.ds(start, tc), sl] for sl in pair_lanes]
        vbs = [vc_ref[pl.ds(start, tc), sl] for sl in pair_lanes]
        return _sb_block(qs, per_head(kbs), per_head(vbs), c, later2, None)

    carry = _sb_sweep(block, past // tc, carry)
    for p, sl in enumerate(pair_lanes):
        o_ref[:, sl] = _pair_select(carry[2 * p][1], carry[2 * p + 1][1]).astype(o_ref.dtype)


def _sb_sample_attn(q, k_cache, v_cache, k_new, v_new):
    nb, t, w = q.shape
    past = k_cache.shape[1]
    pw = SB_PAIRS_PER_STEP * 2 * HEAD_DIM
    tc = min(SB_CACHE_BLOCK, past)
    assert past % tc == 0 and w % pw == 0
    kern = functools.partial(_sb_sample_kernel, tc=tc)
    new_spec = pl.BlockSpec((None, t, pw), lambda b, p: (b, 0, p))
    cache_spec = pl.BlockSpec((None, past, pw), lambda b, p: (b, 0, p))
    return pl.pallas_call(
        kern,
        grid=(nb, w // pw),
        in_specs=[new_spec, cache_spec, cache_spec, new_spec, new_spec],
        out_specs=new_spec,
        out_shape=jax.ShapeDtypeStruct((nb, t, w), BF16),
        compiler_params=_params(2),
        name="sb_sample_attn",
    )(q, k_cache, v_cache, k_new, v_new)


HC = H_C * HEAD_DIM


def _odd_proj_kernel(x_ref, w_ref, q_ref, k32_ref, v32_ref, k16_ref, v16_ref):
    xb = x_ref[...].astype(BF16)
    h = _dot(xb, w_ref[...])
    q_ref[...] = (h[:, 0:HC] * QK_SCALE).astype(BF16)
    k = h[:, HC:2 * HC]
    v = h[:, 2 * HC:3 * HC]
    k32_ref[...] = k
    v32_ref[...] = v
    k16_ref[...] = k.astype(BF16)
    v16_ref[...] = v.astype(BF16)


def _odd_proj(x, w):
    t, d = x.shape
    tm = min(ROW_TILE, t)
    return pl.pallas_call(
        _odd_proj_kernel,
        grid=(t // tm,),
        in_specs=[_rows(tm, d), _resident(w.shape)],
        out_specs=(_rows(tm, HC),) * 5,
        out_shape=(jax.ShapeDtypeStruct((t, HC), BF16), jax.ShapeDtypeStruct((t, HC), F32),
                   jax.ShapeDtypeStruct((t, HC), F32), jax.ShapeDtypeStruct((t, HC), BF16),
                   jax.ShapeDtypeStruct((t, HC), BF16)),
        compiler_params=_params(1, V7X_VMEM_LIMIT_BYTES),
        name="odd_in_proj",
    )(x, w)


def _out_ln_kernel(*refs, n_mix, alpha):
    x_ref = refs[0]
    mix_refs = refs[1:1 + n_mix]
    w_refs = refs[1 + n_mix:1 + 2 * n_mix]
    g_ref, b_ref, o_ref = refs[1 + 2 * n_mix:]
    y = _dot(mix_refs[0][...], w_refs[0][...])
    for m_ref, w_ref in zip(mix_refs[1:], w_refs[1:]):
        y = y + _dot(m_ref[...], w_ref[...])
    o_ref[...] = _layer_norm(alpha * x_ref[...] + y, g_ref[...], b_ref[...])


def _out_ln(x, mixes, ws, g, b, alpha):
    t, d = x.shape
    tm = min(ROW_TILE, t)
    kern = functools.partial(_out_ln_kernel, n_mix=len(mixes), alpha=alpha)
    return pl.pallas_call(
        kern,
        grid=(t // tm,),
        in_specs=([_rows(tm, d)] + [_rows(tm, m.shape[1]) for m in mixes]
                  + [_resident(w.shape) for w in ws] + [_resident(g.shape), _resident(b.shape)]),
        out_specs=_rows(tm, d),
        out_shape=jax.ShapeDtypeStruct((t, d), F32),
        compiler_params=_params(1),
        name="out_proj_ln",
    )(x, *mixes, *ws, g, b)


def _ffn_ln_kernel(x_ref, wu_ref, wd_ref, g_ref, b_ref, o_ref, *, alpha):
    x = x_ref[...]
    h = _dot(x.astype(BF16), wu_ref[...])
    h = jnp.maximum(h, 0.0)
    ff = _dot((h * h).astype(BF16), wd_ref[...])
    o_ref[...] = _layer_norm(alpha * x + ff, g_ref[...], b_ref[...])


def _ffn_ln(x, wu, wd, g, b, alpha):
    t, d = x.shape
    tm = min(ROW_TILE, t)
    kern = functools.partial(_ffn_ln_kernel, alpha=alpha)
    return pl.pallas_call(
        kern,
        grid=(t // tm,),
        in_specs=[_rows(tm, d), _resident(wu.shape), _resident(wd.shape),
                  _resident(g.shape), _resident(b.shape)],
        out_specs=_rows(tm, d),
        out_shape=jax.ShapeDtypeStruct((t, d), F32),
        compiler_params=_params(1, V7X_VMEM_LIMIT_BYTES),
        name="ffn_ln",
    )(x, wu, wd, g, b)


def _rope_tables(pos, reps):
    inv = jnp.power(ROPE_THETA, -jnp.arange(ROPE_HALF, dtype=F32) / ROPE_HALF)
    ang = pos.astype(F32)[:, None] * inv[None, :]
    cos, sin = jnp.cos(ang), jnp.sin(ang)
    n = pos.shape[0]
    z = lambda w: jnp.zeros((n, w), F32)
    tail = LANES - ROPE_LANE0 - ROPE_DIM
    c = jnp.concatenate([jnp.ones((n, ROPE_LANE0), F32), cos, cos, z(tail)], axis=1)
    s1 = jnp.concatenate([z(ROPE_LANE0), -sin, z(ROPE_HALF), z(tail)], axis=1)
    s2 = jnp.concatenate([z(ROPE_LANE0), z(ROPE_HALF), sin, z(tail)], axis=1)
    return tuple(jnp.tile(a, (reps, 1)) for a in (c, s1, s2))


CAST_ROWS = 256


def _cast_kernel(w_ref, o_ref):
    o_ref[...] = w_ref[...].astype(o_ref.dtype)


def _to_bf16(w):
    r, c = w.shape
    return pl.pallas_call(
        _cast_kernel,
        grid=(r // CAST_ROWS,),
        in_specs=[_rows(CAST_ROWS, c)],
        out_specs=_rows(CAST_ROWS, c),
        out_shape=jax.ShapeDtypeStruct((r, c), BF16),
        compiler_params=_params(1),
        name="weight_cast",
    )(w)


def _even_weights(w_in, g_q, w_uq, g_kv, w_ukv, w_out):
    d = w_in.shape[0]
    o_r = Q_LORA + KV_LORA
    tail = LANES - ROPE_LANE0 - ROPE_DIM
    w_in = _to_bf16(w_in)
    w1 = jnp.concatenate(
        [w_in[:, :o_r], jnp.zeros((d, ROPE_LANE0), BF16), w_in[:, o_r:o_r + ROPE_DIM],
         jnp.zeros((d, tail), BF16), w_in[:, o_r + ROPE_DIM:]], axis=1)
    per_head = NOPE_DIM + ROPE_DIM
    wuq = jnp.pad(_to_bf16(w_uq).reshape(Q_LORA, H_A, per_head),
                  ((0, 0), (0, 0), (0, MLA_HEAD_LANES - per_head)))
    wuq = wuq.reshape(Q_LORA, H_A * MLA_HEAD_LANES)
    wkv = _to_bf16(w_ukv).reshape(KV_LORA, H_A, NOPE_DIM + V_DIM)
    wk = jnp.pad(wkv[:, :, :NOPE_DIM], ((0, 0), (0, 0), (0, MLA_HEAD_LANES - NOPE_DIM)))
    wk = wk.reshape(KV_LORA, H_A * MLA_HEAD_LANES)
    wv = wkv[:, :, NOPE_DIM:].reshape(KV_LORA, H_A * V_DIM)
    wo = _to_bf16(w_out)
    wo_a, wo_b = wo[:H_A * V_DIM], wo[H_A * V_DIM:]
    return dict(w1=w1, gq=g_q[None, :], gkv=g_kv[None, :], wuq=wuq, wk=wk, wv=wv,
                wo_a=wo_a, wo_b=wo_b)


def _round_up(n, m):
    return (n + m - 1) // m * m


ATTN_TILE = 256


def _even_mixer(x, nb, t, ropes, w, table, cache):
    qpad, ckv, krw, qb, kb32, vb32, kb16, vb16 = _even_proj(x, w["w1"], w["gq"], w["gkv"], w["wuq"], ropes)
    r3 = lambda a: a.reshape(nb, t, a.shape[-1])
    qpad, qb, kb16, vb16 = map(r3, (qpad, qb, kb16, vb16))
    kb32 = kb32.reshape(nb, t, H_B, HEAD_DIM)
    vb32 = vb32.reshape(nb, t, H_B, HEAD_DIM)
    if cache is None:
        tq = min(ATTN_TILE, t)
        tk = min(2 * tq, t)
        kpad, vt = _kv_up_blocked(ckv, krw, w["wk"], w["wv"].T, nb, t, tk)
        o_a = _mla_prompt_attn(qpad, r3(kpad), vt, tq=tq, tk=tk)
        nkb = min(BAND_PAST // tq + 1, t // tq)
        bias = _band_bias(table, tq=tq, tk=tq, nkb=nkb, base_off=(nkb - 1) * tq,
                          band_mask=True, kv_len=t)
        o_b = _band_attn(qb, kb16, vb16, bias, tq=tq, tk=tq, nkb=nkb)
        rows = min(BAND_PAST, t)
        new_bk, new_bv = kb32[:, t - rows:], vb32[:, t - rows:]
    else:
        c_ckv, c_kr, c_bk, c_bv = cache
        past = c_ckv.shape[1]
        tail = LANES - ROPE_LANE0 - ROPE_DIM
        c_krw = jnp.pad(c_kr, ((0, 0), (0, 0), (ROPE_LANE0, tail)))
        kpad_c, v16_c = _kv_up(c_ckv.reshape(nb * past, KV_LORA), c_krw.reshape(nb * past, LANES),
                               w["wk"], w["wv"])
        kpad, v16 = _kv_up(ckv, krw, w["wk"], w["wv"])
        assert past % CHUNK == 0 and t <= CHUNK
        o_a = _mla_sample_attn(qpad, kpad_c.reshape(nb, past, -1), v16_c.reshape(nb, past, -1),
                               r3(kpad), r3(v16))
        n_past = c_bk.shape[1]
        band_len = n_past + t
        bias = _band_bias(table, tq=t, tk=_round_up(band_len, LANES), nkb=1, base_off=n_past,
                          band_mask=False, kv_len=band_len)[:, 0]
        flat16 = lambda c: c.reshape(nb, n_past, HB).astype(BF16)
        o_b = _band_sample_attn(qb, flat16(c_bk), flat16(c_bv), kb16, vb16,
                                bias[:, :, :n_past], bias[:, :, n_past:band_len])
        rows = min(BAND_PAST, band_len)
        new_bk = jnp.concatenate([c_bk, kb32], axis=1)[:, band_len - rows:]
        new_bv = jnp.concatenate([c_bv, vb32], axis=1)[:, band_len - rows:]
    mixes = (o_a.reshape(nb * t, -1), o_b.reshape(nb * t, -1))
    new_ckv = ckv.reshape(nb, t, KV_LORA)
    new_kr = krw[:, ROPE_LANE0:ROPE_LANE0 + ROPE_DIM].reshape(nb, t, ROPE_DIM)
    return mixes, (w["wo_a"], w["wo_b"]), (new_ckv, new_kr, new_bk, new_bv)


def _odd_mixer(x, nb, t, w_in, w_out, cache):
    q16, k32, v32, k16, v16 = _odd_proj(x, w_in)
    r3 = lambda a: a.reshape(nb, t, HC)
    q16, k16, v16 = map(r3, (q16, k16, v16))
    if cache is None:
        tq = min(ATTN_TILE, t)
        o = _sb_attn(q16, k16, v16, tq=tq, tk=tq, q_off=0)
    else:
        c_k, c_v = cache
        past = c_k.shape[1]
        flat16 = lambda c: c.reshape(nb, past, HC).astype(BF16)
        o = _sb_sample_attn(q16, flat16(c_k), flat16(c_v), k16, v16)
    new_k = k32.reshape(nb, t, H_C, HEAD_DIM)
    new_v = v32.reshape(nb, t, H_C, HEAD_DIM)
    return (o.reshape(nb * t, HC),), (w_out,), (new_k, new_v)


def kernel(x_prompt, x_sample, cache_mla_ckv, cache_mla_krope, cache_band_k, cache_band_v,
           cache_sb_k, cache_sb_v, w_in_ab, g_q_lat, w_uq, g_kv_lat, w_ukv, rel_bias, w_out_ab,
           w_in_c, w_out_c, ln_mix_g, ln_mix_b, ln_ffn_g, ln_ffn_b, w_ff_up, w_ff_down):
    nb_p, t_p, d = x_prompt.shape
    nb_s, t_s, _ = x_sample.shape
    past = cache_mla_ckv.shape[2]
    depth = ln_mix_g.shape[0]
    alpha = (2.0 * depth) ** 0.25
    xp = x_prompt.reshape(nb_p * t_p, d)
    xs = x_sample.reshape(nb_s * t_s, d)
    ropes_p = _rope_tables(jnp.arange(t_p, dtype=jnp.int32), nb_p)
    ropes_s = _rope_tables(past + jnp.arange(t_s, dtype=jnp.int32), nb_s)
    even_p, even_s, odd_p, odd_s = [], [], [], []
    for l in range(depth):
        i = l // 2
        if l % 2 == 0:
            w = _even_weights(w_in_ab[i], g_q_lat[i], w_uq[i], g_kv_lat[i], w_ukv[i], w_out_ab[i])
            mp, wo, new_p = _even_mixer(xp, nb_p, t_p, ropes_p, w, rel_bias[i], None)
            ms, _, new_s = _even_mixer(
                xs, nb_s, t_s, ropes_s, w, rel_bias[i],
                (cache_mla_ckv[i], cache_mla_krope[i], cache_band_k[i], cache_band_v[i]))
            even_p.append(new_p)
            even_s.append(new_s)
        else:
            w_in = _to_bf16(w_in_c[i])
            w_out = _to_bf16(w_out_c[i])
            mp, wo, new_p = _odd_mixer(xp, nb_p, t_p, w_in, w_out, None)
            ms, _, new_s = _odd_mixer(xs, nb_s, t_s, w_in, w_out, (cache_sb_k[i], cache_sb_v[i]))
            odd_p.append(new_p)
            odd_s.append(new_s)
        g1, b1 = ln_mix_g[l][None, :], ln_mix_b[l][None, :]
        g2, b2 = ln_ffn_g[l][None, :], ln_ffn_b[l][None, :]
        wu, wd = _to_bf16(w_ff_up[l]), _to_bf16(w_ff_down[l])
        xp = _ffn_ln(_out_ln(xp, mp, wo, g1, b1, alpha), wu, wd, g2, b2, alpha)
        xs = _ffn_ln(_out_ln(xs, ms, wo, g1, b1, alpha), wu, wd, g2, b2, alpha)
    stack = lambda groups, k: jnp.stack([g[k] for g in groups])
    return (xp.reshape(nb_p, t_p, d), xs.reshape(nb_s, t_s, d),
            stack(even_p, 0), stack(even_p, 1), stack(even_p, 2), stack(even_p, 3),
            stack(odd_p, 0), stack(odd_p, 1),
            stack(even_s, 0), stack(even_s, 1), stack(even_s, 2), stack(even_s, 3),
            stack(odd_s, 0), stack(odd_s, 1))
```

```python
import functools

import jax
import jax.numpy as jnp
from jax import lax
from jax.experimental import pallas as pl
from jax.experimental.pallas import tpu as pltpu

F32 = jnp.float32
BF16 = jnp.bfloat16

CHUNK = 64
CHUNK_SHIFT = 6
HEAD_DIM = 64
H_A = 8
Q_LORA = 768
KV_LORA = 256
NOPE_DIM = 64
ROPE_DIM = 32
V_DIM = 64
ROPE_THETA = 10000.0
MLA_SCALE = (NOPE_DIM + ROPE_DIM) ** -0.5
H_B = 8
PREV_CHUNKS = 8
BAND_PAST = PREV_CHUNKS * CHUNK
REL_MAX = 256
N_REL = REL_MAX + CHUNK
H_C = 16
QK_SCALE = HEAD_DIM ** -0.5
NEG_INF = -1e30

LANES = 128
SUBLANES = 8
V7X_VMEM_LIMIT_BYTES = 56 * 1024 * 1024

MLA_HEAD_LANES = LANES
ROPE_LANE0 = NOPE_DIM
ROPE_HALF = ROPE_DIM // 2

ROW_TILE = 512


def _params(n_axes, vmem_bytes=None):
    return pltpu.CompilerParams(
        dimension_semantics=("parallel",) * n_axes,
        vmem_limit_bytes=vmem_bytes,
    )


def _resident(shape):
    nd = len(shape)
    return pl.BlockSpec(shape, lambda *_: (0,) * nd, pipeline_mode=pl.Buffered(1))


def _rows(tm, width):
    return pl.BlockSpec((tm, width), lambda i: (i, 0))


def _dot(a, b):
    return jnp.dot(a, b, preferred_element_type=F32)


def _dot_nt(a, b):
    return lax.dot_general(a, b, (((1,), (1,)), ((), ())), preferred_element_type=F32)


def _rms(h, g, eps=1e-6):
    return h * lax.rsqrt(jnp.mean(h * h, axis=-1, keepdims=True) + eps) * g


def _layer_norm(r, g, b, eps=1e-5):
    mu = jnp.mean(r, axis=-1, keepdims=True)
    d = r - mu
    var = jnp.mean(d * d, axis=-1, keepdims=True)
    return d * lax.rsqrt(var + eps) * g + b


N_BAND = 3 * H_B * HEAD_DIM
W1_Q0, W1_Q1 = 0, Q_LORA
W1_C0, W1_C1 = W1_Q1, W1_Q1 + KV_LORA
W1_R0, W1_R1 = W1_C1, W1_C1 + LANES
W1_B0, W1_B1 = W1_R1, W1_R1 + N_BAND
HB = H_B * HEAD_DIM


def _even_proj_kernel(x_ref, w1_ref, gq_ref, gkv_ref, wuq_ref, c_ref, s1_ref, s2_ref,
                      qpad_ref, ckv_ref, krw_ref, qb_ref, kb32_ref, vb32_ref, kb16_ref, vb16_ref):
    xb = x_ref[...].astype(BF16)
    c = c_ref[...]
    s1 = s1_ref[...]
    s2 = s2_ref[...]

    def rope(v):
        return (v * c + pltpu.roll(v, LANES - ROPE_HALF, 1) * s1
                + pltpu.roll(v, ROPE_HALF, 1) * s2)

    hq = _dot(xb, w1_ref[:, W1_Q0:W1_Q1])
    qn = _rms(hq, gq_ref[...]).astype(BF16)
    qa = _dot(qn, wuq_ref[...])
    for h in range(H_A):
        sl = slice(h * MLA_HEAD_LANES, (h + 1) * MLA_HEAD_LANES)
        qpad_ref[:, sl] = rope(qa[:, sl]).astype(BF16)

    hc = _dot(xb, w1_ref[:, W1_C0:W1_C1])
    ckv_ref[...] = _rms(hc, gkv_ref[...])

    hk = _dot(xb, w1_ref[:, W1_R0:W1_R1])
    krw_ref[...] = rope(hk)

    hb = _dot(xb, w1_ref[:, W1_B0:W1_B1])
    qb_ref[...] = (hb[:, 0:HB] * QK_SCALE).astype(BF16)
    kb = hb[:, HB:2 * HB]
    vb = hb[:, 2 * HB:3 * HB]
    kb32_ref[...] = kb
    vb32_ref[...] = vb
    kb16_ref[...] = kb.astype(BF16)
    vb16_ref[...] = vb.astype(BF16)


def _even_proj(x, w1, gq, gkv, wuq, rope_tabs):
    t, d = x.shape
    tm = min(ROW_TILE, t)
    c, s1, s2 = rope_tabs
    qw = H_A * MLA_HEAD_LANES
    out_shape = (
        jax.ShapeDtypeStruct((t, qw), BF16),
        jax.ShapeDtypeStruct((t, KV_LORA), F32),
        jax.ShapeDtypeStruct((t, LANES), F32),
        jax.ShapeDtypeStruct((t, HB), BF16),
        jax.ShapeDtypeStruct((t, HB), F32),
        jax.ShapeDtypeStruct((t, HB), F32),
        jax.ShapeDtypeStruct((t, HB), BF16),
        jax.ShapeDtypeStruct((t, HB), BF16),
    )
    return pl.pallas_call(
        _even_proj_kernel,
        grid=(t // tm,),
        in_specs=[_rows(tm, d), _resident(w1.shape), _resident(gq.shape), _resident(gkv.shape),
                  _resident(wuq.shape), _rows(tm, LANES), _rows(tm, LANES), _rows(tm, LANES)],
        out_specs=(_rows(tm, qw), _rows(tm, KV_LORA), _rows(tm, LANES), _rows(tm, HB),
                   _rows(tm, HB), _rows(tm, HB), _rows(tm, HB), _rows(tm, HB)),
        out_shape=out_shape,
        compiler_params=_params(1, V7X_VMEM_LIMIT_BYTES),
        name="even_in_proj",
    )(x, w1, gq, gkv, wuq, c, s1, s2)


def _kv_up_kernel(ckv_ref, krw_ref, wk_ref, wv_ref, kpad_ref, v_ref, *, values_on_rows):
    cb = ckv_ref[...].astype(BF16)
    krw = krw_ref[...]
    k = _dot(cb, wk_ref[...])
    for h in range(H_A):
        sl = slice(h * MLA_HEAD_LANES, (h + 1) * MLA_HEAD_LANES)
        kpad_ref[:, sl] = (k[:, sl] + krw).astype(BF16)
    if values_on_rows:
        v_ref[...] = _dot_nt(wv_ref[...], cb).astype(BF16)
    else:
        v_ref[...] = _dot(cb, wv_ref[...]).astype(BF16)


def _kv_up(ckv, krw, wk, wv):
    t = ckv.shape[0]
    tm = min(ROW_TILE, t)
    kw = H_A * MLA_HEAD_LANES
    vw = H_A * V_DIM
    return pl.pallas_call(
        functools.partial(_kv_up_kernel, values_on_rows=False),
        grid=(t // tm,),
        in_specs=[_rows(tm, KV_LORA), _rows(tm, LANES), _resident(wk.shape), _resident(wv.shape)],
        out_specs=(_rows(tm, kw), _rows(tm, vw)),
        out_shape=(jax.ShapeDtypeStruct((t, kw), BF16), jax.ShapeDtypeStruct((t, vw), BF16)),
        compiler_params=_params(1),
        name="mla_kv_up",
    )(ckv, krw, wk, wv)


def _kv_up_blocked(ckv, krw, wk, wv_t, nb, seq, tk):
    t = ckv.shape[0]
    kw = H_A * MLA_HEAD_LANES
    vw = H_A * V_DIM
    per_seq = seq // tk
    return pl.pallas_call(
        functools.partial(_kv_up_kernel, values_on_rows=True),
        grid=(t // tk,),
        in_specs=[_rows(tk, KV_LORA), _rows(tk, LANES), _resident(wk.shape), _resident(wv_t.shape)],
        out_specs=(_rows(tk, kw),
                   pl.BlockSpec((None, None, vw, tk), lambda i: (i // per_seq, i % per_seq, 0, 0))),
        out_shape=(jax.ShapeDtypeStruct((t, kw), BF16),
                   jax.ShapeDtypeStruct((nb, per_seq, vw, tk), BF16)),
        compiler_params=_params(1),
        name="mla_kv_up_blocked",
    )(ckv, krw, wk, wv_t)


def _pair_select(lo, hi):
    lane = lax.broadcasted_iota(jnp.int32, lo.shape, 1)
    return jnp.where(lane < HEAD_DIM, lo, hi)


def _mla_sample_kernel(q_ref, kc_ref, vc_ref, kn_ref, vn_ref, o_ref):
    n_heads = q_ref.shape[1] // MLA_HEAD_LANES
    hsls = [slice(h * MLA_HEAD_LANES, (h + 1) * MLA_HEAD_LANES) for h in range(n_heads)]
    vsls = [slice(h // 2 * 2 * V_DIM, (h // 2 + 1) * 2 * V_DIM) for h in range(n_heads)]
    segments = ((kc_ref, vc_ref), (kn_ref, vn_ref))
    exp_scale = MLA_SCALE * LOG2_E
    raw = [[_dot_nt(q_ref[:, hsl], k_ref[:, hsl]) for k_ref, _ in segments] for hsl in hsls]
    probs, sums = [], []
    for scores in raw:
        m = functools.reduce(jnp.maximum, [s.max(axis=-1, keepdims=True) for s in scores])
        ps = [jnp.exp2((s - m) * exp_scale) for s in scores]
        sums.append(functools.reduce(jnp.add, [p.sum(axis=-1, keepdims=True) for p in ps]))
        probs.append([p.astype(BF16) for p in ps])
    outs = []
    for vsl, ps, l in zip(vsls, probs, sums):
        acc = functools.reduce(
            jnp.add, [_dot(p, v_ref[:, vsl]) for p, (_, v_ref) in zip(ps, segments)])
        outs.append(acc / l)
    for p in range(n_heads // 2):
        o_ref[:, vsls[2 * p]] = _pair_select(outs[2 * p], outs[2 * p + 1]).astype(o_ref.dtype)


MLA_PAIRS_PER_STEP = 2


def _mla_sample_attn(q, k_cache, v_cache, k_new, v_new):
    nb, t, _ = q.shape
    past = k_cache.shape[1]
    qk_w = MLA_PAIRS_PER_STEP * 2 * MLA_HEAD_LANES
    v_w = MLA_PAIRS_PER_STEP * 2 * V_DIM
    spec = lambda rows, width: pl.BlockSpec((None, rows, width), lambda b, p: (b, 0, p))
    return pl.pallas_call(
        _mla_sample_kernel,
        grid=(nb, H_A * V_DIM // v_w),
        in_specs=[spec(t, qk_w), spec(past, qk_w), spec(past, v_w), spec(t, qk_w), spec(t, v_w)],
        out_specs=spec(t, v_w),
        out_shape=jax.ShapeDtypeStruct((nb, t, H_A * V_DIM), BF16),
        compiler_params=_params(2),
        name="mla_sample_attn",
    )(q, k_cache, v_cache, k_new, v_new)


def _mla_prompt_kernel(q_ref, k_ref, vt_ref, o_ref, *, tq, tk):
    i = pl.program_id(2)
    n_heads = q_ref.shape[1] // MLA_HEAD_LANES
    n_full = (i * tq) // tk
    k_row = lax.broadcasted_iota(jnp.int32, (tk, tq), 0)
    q_col = lax.broadcasted_iota(jnp.int32, (tk, tq), 1)
    q_chunk = (q_col + i * tq) >> CHUNK_SHIFT
    hsls = [slice(h * MLA_HEAD_LANES, (h + 1) * MLA_HEAD_LANES) for h in range(n_heads)]
    vsls = [slice(h // 2 * 2 * V_DIM, (h // 2 + 1) * 2 * V_DIM) for h in range(n_heads)]
    qs = [q_ref[:, hsl] for hsl in hsls]
    exp_scale = MLA_SCALE * LOG2_E

    def scores(j):
        start = pl.multiple_of(j * tk, tk)
        return tuple(_dot_nt(k_ref[pl.ds(start, tk), hsl], q) for q, hsl in zip(qs, hsls))

    def attend(j, ss, carry, masked):
        ps, stats = [], []
        for s, (m, l, _) in zip(ss, carry):
            if masked:
                s = jnp.where(((k_row + j * tk) >> CHUNK_SHIFT) <= q_chunk, s, NEG_INF)
            m_new = jnp.maximum(m, jnp.max(s, axis=0, keepdims=True))
            alpha = jnp.exp2((m - m_new) * exp_scale)
            p = jnp.exp2((s - m_new) * exp_scale)
            ps.append(p.astype(BF16))
            stats.append((m_new, alpha, alpha * l + jnp.sum(p, axis=0, keepdims=True)))
        return tuple(
            (m_new, l, alpha * acc + _dot(vt_ref[j, vsl, :], p))
            for p, vsl, (m_new, alpha, l), (_, _, acc) in zip(ps, vsls, stats, carry))

    init = tuple((jnp.full((1, tq), NEG_INF, F32), jnp.zeros((1, tq), F32),
                  jnp.zeros((2 * V_DIM, tq), F32)) for _ in range(n_heads))
    carry = lax.fori_loop(0, n_full, lambda j, c: attend(j, scores(j), c, False), init)
    carry = attend(n_full, scores(n_full), carry, True)
    outs = [acc / l for _, l, acc in carry]
    v_row = lax.broadcasted_iota(jnp.int32, (2 * V_DIM, tq), 0)
    for p in range(n_heads // 2):
        pair = jnp.where(v_row < V_DIM, outs[2 * p], outs[2 * p + 1])
        o_ref[:, vsls[2 * p]] = pair.T.astype(o_ref.dtype)


def _mla_prompt_attn(q, k, vt, *, tq, tk):
    nb, sq, _ = q.shape
    pairs = H_A // (2 * MLA_PAIRS_PER_STEP)
    qk_w = MLA_PAIRS_PER_STEP * 2 * MLA_HEAD_LANES
    v_w = MLA_PAIRS_PER_STEP * 2 * V_DIM
    assert vt.shape == (nb, sq // tk, H_A * V_DIM, tk)
    kern = functools.partial(_mla_prompt_kernel, tq=tq, tk=tk)
    return pl.pallas_call(
        kern,
        grid=(nb, pairs, sq // tq),
        in_specs=[pl.BlockSpec((None, tq, qk_w), lambda b, p, i: (b, i, p)),
                  pl.BlockSpec((None, sq, qk_w), lambda b, p, i: (b, 0, p)),
                  pl.BlockSpec((None, sq // tk, v_w, tk), lambda b, p, i: (b, 0, p, 0))],
        out_specs=pl.BlockSpec((None, tq, v_w), lambda b, p, i: (b, i, p)),
        out_shape=jax.ShapeDtypeStruct((nb, sq, H_A * V_DIM), BF16),
        compiler_params=_params(3),
        name="mla_prompt_attn",
    )(q, k, vt)


def _half_masks(x):
    lane = lax.broadcasted_iota(jnp.int32, x.shape, 1)
    zero = jnp.zeros_like(x)
    return jnp.where(lane < HEAD_DIM, x, zero), jnp.where(lane < HEAD_DIM, zero, x)


def _band_core(q_ref, o_ref, segments):
    n_pairs = q_ref.shape[1] // (2 * HEAD_DIM)
    pair_lanes = [slice(p * 2 * HEAD_DIM, (p + 1) * 2 * HEAD_DIM) for p in range(n_pairs)]
    chains = [(2 * p + hh, q, sl) for p, sl in enumerate(pair_lanes)
              for hh, q in enumerate(_half_masks(q_ref[:, sl]))]
    raw = [[_dot_nt(q, keys(sl)) for keys, _, _, _ in segments] for _, q, sl in chains]
    probs, sums = [], []
    for (h, _, _), head_raw in zip(chains, raw):
        scores = []
        for s, (_, _, bias, live) in zip(head_raw, segments):
            s = s + bias(h)
            scores.append(s if live is None else jnp.where(live, s, NEG_INF))
        m = functools.reduce(jnp.maximum, [s.max(axis=-1, keepdims=True) for s in scores])
        ps = [jnp.exp(s - m) for s in scores]
        sums.append(functools.reduce(jnp.add, [p.sum(axis=-1, keepdims=True) for p in ps]))
        probs.append([p.astype(BF16) for p in ps])
    outs = []
    for (_, _, sl), ps, l in zip(chains, probs, sums):
        acc = functools.reduce(
            jnp.add, [_dot(p, values(sl)) for p, (_, values, _, _) in zip(ps, segments)])
        outs.append(acc / l)
    for p, sl in enumerate(pair_lanes):
        o_ref[:, sl] = _pair_select(outs[2 * p], outs[2 * p + 1]).astype(o_ref.dtype)


def _band_kernel(q_ref, k_ref, v_ref, bias_ref, o_ref, *, tq, tk, nkb):
    i = pl.program_id(2)
    segments = []
    for d in range(nkb):
        kbi = i - (nkb - 1) + d
        start = pl.multiple_of(jnp.maximum(kbi, 0) * tk, tk)
        segments.append((lambda sl, start=start: k_ref[pl.ds(start, tk), sl],
                         lambda sl, start=start: v_ref[pl.ds(start, tk), sl],
                         lambda h, d=d: bias_ref[h, d], kbi >= 0))
    _band_core(q_ref, o_ref, segments)


def _band_sample_kernel(q_ref, kc_ref, vc_ref, kn_ref, vn_ref, bc_ref, bn_ref, o_ref):
    _band_core(q_ref, o_ref, (
        (lambda sl: kc_ref[:, sl], lambda sl: vc_ref[:, sl], lambda h: bc_ref[h], None),
        (lambda sl: kn_ref[:, sl], lambda sl: vn_ref[:, sl], lambda h: bn_ref[h], None)))


BAND_PAIRS_PER_STEP = 2


def _band_attn(q, k, v, bias, *, tq, tk, nkb):
    nb, sq, w = q.shape
    sk = k.shape[1]
    pw = BAND_PAIRS_PER_STEP * 2 * HEAD_DIM
    heads_per_step = 2 * BAND_PAIRS_PER_STEP
    assert bias.shape == (H_B, nkb, tq, tk) and w % pw == 0
    assert tq == tk or (nkb == 1 and sq == tq and sk == tk)
    kern = functools.partial(_band_kernel, tq=tq, tk=tk, nkb=nkb)
    return pl.pallas_call(
        kern,
        grid=(nb, w // pw, sq // tq),
        in_specs=[pl.BlockSpec((None, tq, pw), lambda b, p, i: (b, i, p)),
                  pl.BlockSpec((None, sk, pw), lambda b, p, i: (b, 0, p)),
                  pl.BlockSpec((None, sk, pw), lambda b, p, i: (b, 0, p)),
                  pl.BlockSpec((heads_per_step, nkb, tq, tk), lambda b, p, i: (p, 0, 0, 0))],
        out_specs=pl.BlockSpec((None, tq, pw), lambda b, p, i: (b, i, p)),
        out_shape=jax.ShapeDtypeStruct((nb, sq, w), BF16),
        compiler_params=_params(3),
        name="band_attn",
    )(q, k, v, bias)


def _band_sample_attn(q, k_cache, v_cache, k_new, v_new, bias_cache, bias_new):
    nb, t, w = q.shape
    n_past = k_cache.shape[1]
    pw = BAND_PAIRS_PER_STEP * 2 * HEAD_DIM
    heads_per_step = 2 * BAND_PAIRS_PER_STEP
    assert bias_cache.shape == (H_B, t, n_past) and bias_new.shape == (H_B, t, t)
    spec = lambda rows: pl.BlockSpec((None, rows, pw), lambda b, p: (b, 0, p))
    bias_spec = lambda cols: pl.BlockSpec((heads_per_step, t, cols), lambda b, p: (p, 0, 0))
    return pl.pallas_call(
        _band_sample_kernel,
        grid=(nb, w // pw),
        in_specs=[spec(t), spec(n_past), spec(n_past), spec(t), spec(t),
                  bias_spec(n_past), bias_spec(t)],
        out_specs=spec(t),
        out_shape=jax.ShapeDtypeStruct((nb, t, w), BF16),
        compiler_params=_params(2),
        name="band_sample_attn",
    )(q, k_cache, v_cache, k_new, v_new, bias_cache, bias_new)


def _band_bias_kernel(tab_ref, o_ref, *, tq, tk, nkb, base_off, band_mask, kv_len):
    h = pl.program_id(0)
    d = pl.program_id(1)
    row = lax.broadcasted_iota(jnp.int32, (tq, tk), 0)
    col = lax.broadcasted_iota(jnp.int32, (tq, tk), 1)
    width = _round_up(tq + tk - 1, LANES)
    u = lax.broadcasted_iota(jnp.int32, (SUBLANES, width), 1)
    col_minus_row = jnp.where(u < tk, u, u - width)
    dist = (base_off - d * tk) - col_minus_row
    idx = jnp.clip(dist, -(CHUNK - 1), REL_MAX) + (CHUNK - 1)

    def body(r, acc):
        return jnp.where(idx == r, tab_ref[h, r], acc)

    line = lax.fori_loop(0, N_REL, body, jnp.zeros((SUBLANES, width), F32))
    spread = jnp.broadcast_to(line[0:1, :], (tq, width))
    bias = pltpu.roll(spread, 0, 1, stride=1, stride_axis=0)[:, :tk]
    if band_mask:
        q_chunk = ((nkb - 1) * tk + row) >> CHUNK_SHIFT
        k_chunk = (d * tk + col) >> CHUNK_SHIFT
        gap = q_chunk - k_chunk
        valid = (gap >= 0) & (gap <= PREV_CHUNKS)
    else:
        valid = (d * tk + col) < kv_len
    o_ref[...] = jnp.where(valid, bias, NEG_INF)


def _band_bias(table, *, tq, tk, nkb, base_off, band_mask, kv_len):
    kern = functools.partial(_band_bias_kernel, tq=tq, tk=tk, nkb=nkb, base_off=base_off,
                             band_mask=band_mask, kv_len=kv_len)
    return pl.pallas_call(
        kern,
        grid=(H_B, nkb),
        in_specs=[pl.BlockSpec(memory_space=pltpu.SMEM)],
        out_specs=pl.BlockSpec((None, None, tq, tk), lambda h, d: (h, d, 0, 0)),
        out_shape=jax.ShapeDtypeStruct((H_B, nkb, tq, tk), F32),
        compiler_params=_params(2),
        name="band_bias",
    )(table)


SB_DEAD_TAIL = -110.0
LOG2_E = 1.4426950408889634


def _sb_later2(tk):
    ur = lax.broadcasted_iota(jnp.int32, (tk, tk), 0)
    uc = lax.broadcasted_iota(jnp.int32, (tk, tk), 1)
    later = jnp.where(ur > uc, -1.0, 0.0).astype(BF16)
    return jnp.concatenate([later, later], axis=0)


def _sb_init(tq, n_heads):
    return tuple((jnp.zeros((tq, 1), F32), jnp.zeros((tq, 2 * HEAD_DIM), F32))
                 for _ in range(n_heads))


def _sb_blocks(qs, blocks, carry):
    zs = [[_dot_nt(q, kb) for q, kb in zip(qs, kbs)] for kbs, _, _, _ in blocks]
    sps, log_sigs = [], []
    for block_zs, (_, _, _, valid) in zip(zs, blocks):
        block_sps, block_log_sigs = [], []
        for z in block_zs:
            t = jnp.log(1.0 + jnp.exp2(jnp.abs(z) * -LOG2_E))
            sp = jnp.maximum(z, 0.0) + t
            block_log_sigs.append(jnp.minimum(z, 0.0) - t)
            block_sps.append(sp if valid is None else jnp.where(valid, sp, 0.0))
        sps.append(block_sps)
        log_sigs.append(block_log_sigs)
    inner = []
    for block_sps, (_, _, later2, _) in zip(sps, blocks):
        block_inner = []
        for sp in block_sps:
            sp_hi = sp.astype(BF16)
            sp_lo = (sp - sp_hi.astype(F32)).astype(BF16)
            block_inner.append(_dot(jnp.concatenate([sp_hi, sp_lo], axis=1), later2))
        inner.append(block_inner)
    out = []
    for h, (tail0, acc) in enumerate(carry):
        for b, (_, vbs, _, valid) in enumerate(blocks):
            w = jnp.exp(log_sigs[b][h] + (inner[b][h] + tail0))
            if valid is not None:
                w = jnp.where(valid, w, 0.0)
            acc = acc + _dot(w.astype(BF16), vbs[h])
            tail0 = tail0 - jnp.sum(sps[b][h], axis=-1, keepdims=True)
        out.append((tail0, acc))
    return tuple(out)


def _sb_block(qs, kbs, vbs, carry, later2, valid):
    return _sb_blocks(qs, [(kbs, vbs, later2, valid)], carry)


def _sb_sweep(block, n_blocks, carry):
    def alive(c):
        return functools.reduce(jnp.maximum, [jnp.max(t0) for t0, _ in c]) > SB_DEAD_TAIL

    def cond(state):
        j, live, _ = state
        return (j >= 0) & live

    def body(state):
        j, _, c = state
        c = block(j, c)
        return j - 1, alive(c), c

    return lax.while_loop(cond, body, (n_blocks - 1, alive(carry), carry))[2]


def _sb_kernel(q_ref, k_ref, v_ref, o_ref, *, tq):
    sq = q_ref.shape[0]
    n_pairs = q_ref.shape[1] // (2 * HEAD_DIM)
    pair_lanes = [slice(p * 2 * HEAD_DIM, (p + 1) * 2 * HEAD_DIM) for p in range(n_pairs)]
    row = lax.broadcasted_iota(jnp.int32, (tq, tq), 0)
    col = lax.broadcasted_iota(jnp.int32, (tq, tq), 1)
    below_diagonal = col < row
    later2 = _sb_later2(tq)

    def tile(i, first):
        q_start = pl.multiple_of(i * tq, tq)
        chains = [(q, sl) for sl in pair_lanes
                  for q in _half_masks(q_ref[pl.ds(q_start, tq), sl])]
        qs = [q for q, _ in chains]

        def block(j, valid):
            start = pl.multiple_of(j * tq, tq)
            return ([k_ref[pl.ds(start, tq), sl] for _, sl in chains],
                    [v_ref[pl.ds(start, tq), sl] for _, sl in chains], later2, valid)

        init = _sb_init(tq, len(chains))
        if first:
            carry = _sb_blocks(qs, [block(i, below_diagonal)], init)
        else:
            carry = _sb_blocks(qs, [block(i, below_diagonal), block(i - 1, None)], init)
            carry = _sb_sweep(lambda j, c: _sb_blocks(qs, [block(j, None)], c), i - 1, carry)
        for p, sl in enumerate(pair_lanes):
            o_ref[pl.ds(q_start, tq), sl] = _pair_select(
                carry[2 * p][1], carry[2 * p + 1][1]).astype(o_ref.dtype)

    tile(0, True)

    def later_tile(i, _):
        tile(i, False)
        return 0

    lax.fori_loop(1, sq // tq, later_tile, 0)


SB_PAIRS_PER_STEP = 2


def _sb_attn(q, k, v, *, tq):
    nb, sq, w = q.shape
    sk = k.shape[1]
    pw = SB_PAIRS_PER_STEP * 2 * HEAD_DIM
    assert sk == sq and sq % tq == 0 and w % pw == 0
    kern = functools.partial(_sb_kernel, tq=tq)
    return pl.pallas_call(
        kern,
        grid=(nb, w // pw),
        in_specs=[pl.BlockSpec((None, sq, pw), lambda b, p: (b, 0, p)),
                  pl.BlockSpec((None, sk, pw), lambda b, p: (b, 0, p)),
                  pl.BlockSpec((None, sk, pw), lambda b, p: (b, 0, p))],
        out_specs=pl.BlockSpec((None, sq, pw), lambda b, p: (b, 0, p)),
        out_shape=jax.ShapeDtypeStruct((nb, sq, w), BF16),
        compiler_params=_params(2),
        name="sb_attn",
    )(q, k, v)


SB_CACHE_BLOCK = 256


def _sb_sample_kernel(q_ref, kc_ref, vc_ref, kn_ref, vn_ref, o_ref, *, tc):
    t = q_ref.shape[0]
    past = kc_ref.shape[0]
    n_pairs = q_ref.shape[1] // (2 * HEAD_DIM)
    pair_lanes = [slice(p * 2 * HEAD_DIM, (p + 1) * 2 * HEAD_DIM) for p in range(n_pairs)]
    qs = [q for sl in pair_lanes for q in _half_masks(q_ref[:, sl])]
    per_head = lambda blocks: [b for b in blocks for _ in range(2)]
    row = lax.broadcasted_iota(jnp.int32, (t, t), 0)
    col = lax.broadcasted_iota(jnp.int32, (t, t), 1)
    carry = _sb_block(qs, per_head([kn_ref[:, sl] for sl in pair_lanes]),
                      per_head([vn_ref[:, sl] for sl in pair_lanes]),
                      _sb_init(t, len(qs)), _sb_later2(t), col < row)
    later2 = _sb_later2(tc)

    def block(j, c):
        start = pl.multiple_of(j * tc, tc)
        kbs = [kc_ref[pl.ds(start, tc), sl] for sl in pair_lanes]
        vbs = [vc_ref[pl.ds(start, tc), sl] for sl in pair_lanes]
        return _sb_block(qs, per_head(kbs), per_head(vbs), c, later2, None)

    carry = _sb_sweep(block, past // tc, carry)
    for p, sl in enumerate(pair_lanes):
        o_ref[:, sl] = _pair_select(carry[2 * p][1], carry[2 * p + 1][1]).astype(o_ref.dtype)


def _sb_sample_attn(q, k_cache, v_cache, k_new, v_new):
    nb, t, w = q.shape
    past = k_cache.shape[1]
    pw = SB_PAIRS_PER_STEP * 2 * HEAD_DIM
    tc = min(SB_CACHE_BLOCK, past)
    assert past % tc == 0 and w % pw == 0
    kern = functools.partial(_sb_sample_kernel, tc=tc)
    new_spec = pl.BlockSpec((None, t, pw), lambda b, p: (b, 0, p))
    cache_spec = pl.BlockSpec((None, past, pw), lambda b, p: (b, 0, p))
    return pl.pallas_call(
        kern,
        grid=(nb, w // pw),
        in_specs=[new_spec, cache_spec, cache_spec, new_spec, new_spec],
        out_specs=new_spec,
        out_shape=jax.ShapeDtypeStruct((nb, t, w), BF16),
        compiler_params=_params(2),
        name="sb_sample_attn",
    )(q, k_cache, v_cache, k_new, v_new)


HC = H_C * HEAD_DIM


def _odd_proj_kernel(x_ref, w_ref, q_ref, k32_ref, v32_ref, k16_ref, v16_ref):
    xb = x_ref[...].astype(BF16)
    h = _dot(xb, w_ref[...])
    q_ref[...] = (h[:, 0:HC] * QK_SCALE).astype(BF16)
    k = h[:, HC:2 * HC]
    v = h[:, 2 * HC:3 * HC]
    k32_ref[...] = k
    v32_ref[...] = v
    k16_ref[...] = k.astype(BF16)
    v16_ref[...] = v.astype(BF16)


def _odd_proj(x, w):
    t, d = x.shape
    tm = min(ROW_TILE, t)
    return pl.pallas_call(
        _odd_proj_kernel,
        grid=(t // tm,),
        in_specs=[_rows(tm, d), _resident(w.shape)],
        out_specs=(_rows(tm, HC),) * 5,
        out_shape=(jax.ShapeDtypeStruct((t, HC), BF16), jax.ShapeDtypeStruct((t, HC), F32),
                   jax.ShapeDtypeStruct((t, HC), F32), jax.ShapeDtypeStruct((t, HC), BF16),
                   jax.ShapeDtypeStruct((t, HC), BF16)),
        compiler_params=_params(1, V7X_VMEM_LIMIT_BYTES),
        name="odd_in_proj",
    )(x, w)


def _post_block_kernel(*refs, n_mix, alpha):
    x_ref = refs[0]
    mix_refs = refs[1:1 + n_mix]
    w_refs = refs[1 + n_mix:1 + 2 * n_mix]
    g1_ref, b1_ref, wu_ref, wd_ref, g2_ref, b2_ref, o_ref = refs[1 + 2 * n_mix:]
    y = _dot(mix_refs[0][...], w_refs[0][...])
    for m_ref, w_ref in zip(mix_refs[1:], w_refs[1:]):
        y = y + _dot(m_ref[...], w_ref[...])
    x = _layer_norm(alpha * x_ref[...] + y, g1_ref[...], b1_ref[...])
    h = _dot(x.astype(BF16), wu_ref[...])
    h = jnp.maximum(h, 0.0)
    ff = _dot((h * h).astype(BF16), wd_ref[...])
    o_ref[...] = _layer_norm(alpha * x + ff, g2_ref[...], b2_ref[...])


def _post_block(x, mixes, ws, g1, b1, wu, wd, g2, b2, alpha):
    t, d = x.shape
    tm = min(ROW_TILE, t)
    kern = functools.partial(_post_block_kernel, n_mix=len(mixes), alpha=alpha)
    consts = list(ws) + [g1, b1, wu, wd, g2, b2]
    return pl.pallas_call(
        kern,
        grid=(t // tm,),
        in_specs=([_rows(tm, d)] + [_rows(tm, m.shape[1]) for m in mixes]
                  + [_resident(c.shape) for c in consts]),
        out_specs=_rows(tm, d),
        out_shape=jax.ShapeDtypeStruct((t, d), F32),
        compiler_params=_params(1, V7X_VMEM_LIMIT_BYTES),
        name="post_block",
    )(x, *mixes, *consts)


def _rope_tables(pos, reps):
    inv = jnp.power(ROPE_THETA, -jnp.arange(ROPE_HALF, dtype=F32) / ROPE_HALF)
    ang = pos.astype(F32)[:, None] * inv[None, :]
    cos, sin = jnp.cos(ang), jnp.sin(ang)
    n = pos.shape[0]
    z = lambda w: jnp.zeros((n, w), F32)
    tail = LANES - ROPE_LANE0 - ROPE_DIM
    c = jnp.concatenate([jnp.ones((n, ROPE_LANE0), F32), cos, cos, z(tail)], axis=1)
    s1 = jnp.concatenate([z(ROPE_LANE0), -sin, z(ROPE_HALF), z(tail)], axis=1)
    s2 = jnp.concatenate([z(ROPE_LANE0), z(ROPE_HALF), sin, z(tail)], axis=1)
    return tuple(jnp.tile(a, (reps, 1)) for a in (c, s1, s2))


CAST_ROWS = 256


def _cast_kernel(w_ref, o_ref):
    o_ref[...] = w_ref[...].astype(o_ref.dtype)


def _to_bf16(w):
    r, c = w.shape
    return pl.pallas_call(
        _cast_kernel,
        grid=(r // CAST_ROWS,),
        in_specs=[_rows(CAST_ROWS, c)],
        out_specs=_rows(CAST_ROWS, c),
        out_shape=jax.ShapeDtypeStruct((r, c), BF16),
        compiler_params=_params(1),
        name="weight_cast",
    )(w)


def _even_weights(w_in, g_q, w_uq, g_kv, w_ukv, w_out):
    d = w_in.shape[0]
    o_r = Q_LORA + KV_LORA
    tail = LANES - ROPE_LANE0 - ROPE_DIM
    w_in = _to_bf16(w_in)
    w1 = jnp.concatenate(
        [w_in[:, :o_r], jnp.zeros((d, ROPE_LANE0), BF16), w_in[:, o_r:o_r + ROPE_DIM],
         jnp.zeros((d, tail), BF16), w_in[:, o_r + ROPE_DIM:]], axis=1)
    per_head = NOPE_DIM + ROPE_DIM
    wuq = jnp.pad(_to_bf16(w_uq).reshape(Q_LORA, H_A, per_head),
                  ((0, 0), (0, 0), (0, MLA_HEAD_LANES - per_head)))
    wuq = wuq.reshape(Q_LORA, H_A * MLA_HEAD_LANES)
    wkv = _to_bf16(w_ukv).reshape(KV_LORA, H_A, NOPE_DIM + V_DIM)
    wk = jnp.pad(wkv[:, :, :NOPE_DIM], ((0, 0), (0, 0), (0, MLA_HEAD_LANES - NOPE_DIM)))
    wk = wk.reshape(KV_LORA, H_A * MLA_HEAD_LANES)
    wv = wkv[:, :, NOPE_DIM:].reshape(KV_LORA, H_A * V_DIM)
    wo = _to_bf16(w_out)
    wo_a, wo_b = wo[:H_A * V_DIM], wo[H_A * V_DIM:]
    return dict(w1=w1, gq=g_q[None, :], gkv=g_kv[None, :], wuq=wuq, wk=wk, wv=wv,
                wo_a=wo_a, wo_b=wo_b)


def _round_up(n, m):
    return (n + m - 1) // m * m


ATTN_TILE = 256


def _even_mixer(x, nb, t, ropes, w, table, cache):
    qpad, ckv, krw, qb, kb32, vb32, kb16, vb16 = _even_proj(x, w["w1"], w["gq"], w["gkv"], w["wuq"], ropes)
    r3 = lambda a: a.reshape(nb, t, a.shape[-1])
    qpad, qb, kb16, vb16 = map(r3, (qpad, qb, kb16, vb16))
    kb32 = kb32.reshape(nb, t, H_B, HEAD_DIM)
    vb32 = vb32.reshape(nb, t, H_B, HEAD_DIM)
    if cache is None:
        tq = min(ATTN_TILE, t)
        tk = min(2 * tq, t)
        kpad, vt = _kv_up_blocked(ckv, krw, w["wk"], w["wv"].T, nb, t, tk)
        o_a = _mla_prompt_attn(qpad, r3(kpad), vt, tq=tq, tk=tk)
        nkb = min(BAND_PAST // tq + 1, t // tq)
        bias = _band_bias(table, tq=tq, tk=tq, nkb=nkb, base_off=(nkb - 1) * tq,
                          band_mask=True, kv_len=t)
        o_b = _band_attn(qb, kb16, vb16, bias, tq=tq, tk=tq, nkb=nkb)
        rows = min(BAND_PAST, t)
        new_bk, new_bv = kb32[:, t - rows:], vb32[:, t - rows:]
    else:
        c_ckv, c_kr, c_bk, c_bv = cache
        past = c_ckv.shape[1]
        tail = LANES - ROPE_LANE0 - ROPE_DIM
        c_krw = jnp.pad(c_kr, ((0, 0), (0, 0), (ROPE_LANE0, tail)))
        kpad_c, v16_c = _kv_up(c_ckv.reshape(nb * past, KV_LORA), c_krw.reshape(nb * past, LANES),
                               w["wk"], w["wv"])
        kpad, v16 = _kv_up(ckv, krw, w["wk"], w["wv"])
        assert past % CHUNK == 0 and t <= CHUNK
        o_a = _mla_sample_attn(qpad, kpad_c.reshape(nb, past, -1), v16_c.reshape(nb, past, -1),
                               r3(kpad), r3(v16))
        n_past = c_bk.shape[1]
        band_len = n_past + t
        bias = _band_bias(table, tq=t, tk=_round_up(band_len, LANES), nkb=1, base_off=n_past,
                          band_mask=False, kv_len=band_len)[:, 0]
        flat16 = lambda c: c.reshape(nb, n_past, HB).astype(BF16)
        o_b = _band_sample_attn(qb, flat16(c_bk), flat16(c_bv), kb16, vb16,
                                bias[:, :, :n_past], bias[:, :, n_past:band_len])
        rows = min(BAND_PAST, band_len)
        new_bk = jnp.concatenate([c_bk, kb32], axis=1)[:, band_len - rows:]
        new_bv = jnp.concatenate([c_bv, vb32], axis=1)[:, band_len - rows:]
    mixes = (o_a.reshape(nb * t, -1), o_b.reshape(nb * t, -1))
    new_ckv = ckv.reshape(nb, t, KV_LORA)
    new_kr = krw[:, ROPE_LANE0:ROPE_LANE0 + ROPE_DIM].reshape(nb, t, ROPE_DIM)
    return mixes, (w["wo_a"], w["wo_b"]), (new_ckv, new_kr, new_bk, new_bv)


def _odd_mixer(x, nb, t, w_in, w_out, cache):
    q16, k32, v32, k16, v16 = _odd_proj(x, w_in)
    r3 = lambda a: a.reshape(nb, t, HC)
    q16, k16, v16 = map(r3, (q16, k16, v16))
    if cache is None:
        tq = min(ATTN_TILE, t)
        o = _sb_attn(q16, k16, v16, tq=tq)
    else:
        c_k, c_v = cache
        past = c_k.shape[1]
        flat16 = lambda c: c.reshape(nb, past, HC).astype(BF16)
        o = _sb_sample_attn(q16, flat16(c_k), flat16(c_v), k16, v16)
    new_k = k32.reshape(nb, t, H_C, HEAD_DIM)
    new_v = v32.reshape(nb, t, H_C, HEAD_DIM)
    return (o.reshape(nb * t, HC),), (w_out,), (new_k, new_v)


def kernel(x_prompt, x_sample, cache_mla_ckv, cache_mla_krope, cache_band_k, cache_band_v,
           cache_sb_k, cache_sb_v, w_in_ab, g_q_lat, w_uq, g_kv_lat, w_ukv, rel_bias, w_out_ab,
           w_in_c, w_out_c, ln_mix_g, ln_mix_b, ln_ffn_g, ln_ffn_b, w_ff_up, w_ff_down):
    nb_p, t_p, d = x_prompt.shape
    nb_s, t_s, _ = x_sample.shape
    past = cache_mla_ckv.shape[2]
    depth = ln_mix_g.shape[0]
    alpha = (2.0 * depth) ** 0.25
    xp = x_prompt.reshape(nb_p * t_p, d)
    xs = x_sample.reshape(nb_s * t_s, d)
    ropes_p = _rope_tables(jnp.arange(t_p, dtype=jnp.int32), nb_p)
    ropes_s = _rope_tables(past + jnp.arange(t_s, dtype=jnp.int32), nb_s)
    even_p, even_s, odd_p, odd_s = [], [], [], []
    for l in range(depth):
        i = l // 2
        if l % 2 == 0:
            w = _even_weights(w_in_ab[i], g_q_lat[i], w_uq[i], g_kv_lat[i], w_ukv[i], w_out_ab[i])
            mp, wo, new_p = _even_mixer(xp, nb_p, t_p, ropes_p, w, rel_bias[i], None)
            ms, _, new_s = _even_mixer(
                xs, nb_s, t_s, ropes_s, w, rel_bias[i],
                (cache_mla_ckv[i], cache_mla_krope[i], cache_band_k[i], cache_band_v[i]))
            even_p.append(new_p)
            even_s.append(new_s)
        else:
            w_in = _to_bf16(w_in_c[i])
            w_out = _to_bf16(w_out_c[i])
            mp, wo, new_p = _odd_mixer(xp, nb_p, t_p, w_in, w_out, None)
            ms, _, new_s = _odd_mixer(xs, nb_s, t_s, w_in, w_out, (cache_sb_k[i], cache_sb_v[i]))
            odd_p.append(new_p)
            odd_s.append(new_s)
        g1, b1 = ln_mix_g[l][None, :], ln_mix_b[l][None, :]
        g2, b2 = ln_ffn_g[l][None, :], ln_ffn_b[l][None, :]
        wu, wd = _to_bf16(w_ff_up[l]), _to_bf16(w_ff_down[l])
        xp = _post_block(xp, mp, wo, g1, b1, wu, wd, g2, b2, alpha)
        xs = _post_block(xs, ms, wo, g1, b1, wu, wd, g2, b2, alpha)
    stack = lambda groups, k: jnp.stack([g[k] for g in groups])
    return (xp.reshape(nb_p, t_p, d), xs.reshape(nb_s, t_s, d),
            stack(even_p, 0), stack(even_p, 1), stack(even_p, 2), stack(even_p, 3),
            stack(odd_p, 0), stack(odd_p, 1),
            stack(even_s, 0), stack(even_s, 1), stack(even_s, 2), stack(even_s, 3),
            stack(odd_s, 0), stack(odd_s, 1))
```

```python
import functools

import jax
import jax.numpy as jnp
from jax import lax
from jax.experimental import pallas as pl
from jax.experimental.pallas import tpu as pltpu

F32 = jnp.float32
BF16 = jnp.bfloat16

CHUNK = 64
CHUNK_SHIFT = 6
HEAD_DIM = 64
H_A = 8
Q_LORA = 768
KV_LORA = 256
NOPE_DIM = 64
ROPE_DIM = 32
V_DIM = 64
ROPE_THETA = 10000.0
MLA_SCALE = (NOPE_DIM + ROPE_DIM) ** -0.5
H_B = 8
PREV_CHUNKS = 8
BAND_PAST = PREV_CHUNKS * CHUNK
REL_MAX = 256
N_REL = REL_MAX + CHUNK
H_C = 16
QK_SCALE = HEAD_DIM ** -0.5
NEG_INF = -1e30

LANES = 128
SUBLANES = 8
V7X_VMEM_LIMIT_BYTES = 56 * 1024 * 1024

MLA_HEAD_LANES = LANES
ROPE_LANE0 = NOPE_DIM
ROPE_HALF = ROPE_DIM // 2

ROW_TILE = 512


def _params(n_axes, vmem_bytes=None):
    return pltpu.CompilerParams(
        dimension_semantics=("parallel",) * n_axes,
        vmem_limit_bytes=vmem_bytes,
    )


def _resident(shape):
    nd = len(shape)
    return pl.BlockSpec(shape, lambda *_: (0,) * nd, pipeline_mode=pl.Buffered(1))


def _rows(tm, width):
    return pl.BlockSpec((tm, width), lambda i: (i, 0))


def _dot(a, b):
    return jnp.dot(a, b, preferred_element_type=F32)


def _dot_nt(a, b):
    return lax.dot_general(a, b, (((1,), (1,)), ((), ())), preferred_element_type=F32)


def _rms(h, g, eps=1e-6):
    return h * lax.rsqrt(jnp.mean(h * h, axis=-1, keepdims=True) + eps) * g


def _layer_norm(r, g, b, eps=1e-5):
    mu = jnp.mean(r, axis=-1, keepdims=True)
    d = r - mu
    var = jnp.mean(d * d, axis=-1, keepdims=True)
    return d * lax.rsqrt(var + eps) * g + b


N_BAND = 3 * H_B * HEAD_DIM
W1_Q0, W1_Q1 = 0, Q_LORA
W1_C0, W1_C1 = W1_Q1, W1_Q1 + KV_LORA
W1_R0, W1_R1 = W1_C1, W1_C1 + LANES
W1_B0, W1_B1 = W1_R1, W1_R1 + N_BAND
HB = H_B * HEAD_DIM


def _even_proj_kernel(x_ref, w1_ref, gq_ref, gkv_ref, wuq_ref, c_ref, s1_ref, s2_ref,
                      qpad_ref, ckv_ref, krw_ref, qb_ref, kb32_ref, vb32_ref, kb16_ref, vb16_ref):
    xb = x_ref[...].astype(BF16)
    c = c_ref[...]
    s1 = s1_ref[...]
    s2 = s2_ref[...]

    def rope(v):
        return (v * c + pltpu.roll(v, LANES - ROPE_HALF, 1) * s1
                + pltpu.roll(v, ROPE_HALF, 1) * s2)

    hq = _dot(xb, w1_ref[:, W1_Q0:W1_Q1])
    qn = _rms(hq, gq_ref[...]).astype(BF16)
    qa = _dot(qn, wuq_ref[...])
    for h in range(H_A):
        sl = slice(h * MLA_HEAD_LANES, (h + 1) * MLA_HEAD_LANES)
        qpad_ref[:, sl] = rope(qa[:, sl]).astype(BF16)

    hc = _dot(xb, w1_ref[:, W1_C0:W1_C1])
    ckv_ref[...] = _rms(hc, gkv_ref[...])

    hk = _dot(xb, w1_ref[:, W1_R0:W1_R1])
    krw_ref[...] = rope(hk)

    hb = _dot(xb, w1_ref[:, W1_B0:W1_B1])
    qb_ref[...] = (hb[:, 0:HB] * QK_SCALE).astype(BF16)
    kb = hb[:, HB:2 * HB]
    vb = hb[:, 2 * HB:3 * HB]
    kb32_ref[...] = kb
    vb32_ref[...] = vb
    kb16_ref[...] = kb.astype(BF16)
    vb16_ref[...] = vb.astype(BF16)


def _even_proj(x, w1, gq, gkv, wuq, rope_tabs):
    t, d = x.shape
    tm = min(ROW_TILE, t)
    c, s1, s2 = rope_tabs
    period_tiles = c.shape[0] // tm
    assert c.shape[0] % tm == 0
    rope_spec = pl.BlockSpec((tm, LANES), lambda i: (i % period_tiles, 0))
    qw = H_A * MLA_HEAD_LANES
    out_shape = (
        jax.ShapeDtypeStruct((t, qw), BF16),
        jax.ShapeDtypeStruct((t, KV_LORA), F32),
        jax.ShapeDtypeStruct((t, LANES), F32),
        jax.ShapeDtypeStruct((t, HB), BF16),
        jax.ShapeDtypeStruct((t, HB), F32),
        jax.ShapeDtypeStruct((t, HB), F32),
        jax.ShapeDtypeStruct((t, HB), BF16),
        jax.ShapeDtypeStruct((t, HB), BF16),
    )
    return pl.pallas_call(
        _even_proj_kernel,
        grid=(t // tm,),
        in_specs=[_rows(tm, d), _resident(w1.shape), _resident(gq.shape), _resident(gkv.shape),
                  _resident(wuq.shape), rope_spec, rope_spec, rope_spec],
        out_specs=(_rows(tm, qw), _rows(tm, KV_LORA), _rows(tm, LANES), _rows(tm, HB),
                   _rows(tm, HB), _rows(tm, HB), _rows(tm, HB), _rows(tm, HB)),
        out_shape=out_shape,
        compiler_params=_params(1, V7X_VMEM_LIMIT_BYTES),
        name="even_in_proj",
    )(x, w1, gq, gkv, wuq, c, s1, s2)


def _kv_up_kernel(ckv_ref, krw_ref, wk_ref, wv_ref, kpad_ref, v_ref, *, values_on_rows):
    cb = ckv_ref[...].astype(BF16)
    krw = krw_ref[...]
    k = _dot(cb, wk_ref[...])
    for h in range(H_A):
        sl = slice(h * MLA_HEAD_LANES, (h + 1) * MLA_HEAD_LANES)
        kpad_ref[:, sl] = (k[:, sl] + krw).astype(BF16)
    if values_on_rows:
        v_ref[...] = _dot_nt(wv_ref[...], cb).astype(BF16)
    else:
        v_ref[...] = _dot(cb, wv_ref[...]).astype(BF16)


def _kv_up(ckv, krw, wk, wv):
    t = ckv.shape[0]
    tm = min(ROW_TILE, t)
    kw = H_A * MLA_HEAD_LANES
    vw = H_A * V_DIM
    return pl.pallas_call(
        functools.partial(_kv_up_kernel, values_on_rows=False),
        grid=(t // tm,),
        in_specs=[_rows(tm, KV_LORA), _rows(tm, LANES), _resident(wk.shape), _resident(wv.shape)],
        out_specs=(_rows(tm, kw), _rows(tm, vw)),
        out_shape=(jax.ShapeDtypeStruct((t, kw), BF16), jax.ShapeDtypeStruct((t, vw), BF16)),
        compiler_params=_params(1),
        name="mla_kv_up",
    )(ckv, krw, wk, wv)


def _kv_up_blocked(ckv, krw, wk, wv_t, nb, seq, tk):
    t = ckv.shape[0]
    kw = H_A * MLA_HEAD_LANES
    vw = H_A * V_DIM
    per_seq = seq // tk
    return pl.pallas_call(
        functools.partial(_kv_up_kernel, values_on_rows=True),
        grid=(t // tk,),
        in_specs=[_rows(tk, KV_LORA), _rows(tk, LANES), _resident(wk.shape), _resident(wv_t.shape)],
        out_specs=(_rows(tk, kw),
                   pl.BlockSpec((None, None, vw, tk), lambda i: (i // per_seq, i % per_seq, 0, 0))),
        out_shape=(jax.ShapeDtypeStruct((t, kw), BF16),
                   jax.ShapeDtypeStruct((nb, per_seq, vw, tk), BF16)),
        compiler_params=_params(1),
        name="mla_kv_up_blocked",
    )(ckv, krw, wk, wv_t)


def _pair_select(lo, hi):
    lane = lax.broadcasted_iota(jnp.int32, lo.shape, 1)
    return jnp.where(lane < HEAD_DIM, lo, hi)


def _mla_sample_kernel(q_ref, kc_ref, vc_ref, kn_ref, vn_ref, o_ref):
    n_heads = q_ref.shape[1] // MLA_HEAD_LANES
    hsls = [slice(h * MLA_HEAD_LANES, (h + 1) * MLA_HEAD_LANES) for h in range(n_heads)]
    vsls = [slice(h // 2 * 2 * V_DIM, (h // 2 + 1) * 2 * V_DIM) for h in range(n_heads)]
    segments = ((kc_ref, vc_ref), (kn_ref, vn_ref))
    exp_scale = MLA_SCALE * LOG2_E
    raw = [[_dot_nt(q_ref[:, hsl], k_ref[:, hsl]) for k_ref, _ in segments] for hsl in hsls]
    probs, sums = [], []
    for scores in raw:
        m = functools.reduce(jnp.maximum, [s.max(axis=-1, keepdims=True) for s in scores])
        ps = [jnp.exp2((s - m) * exp_scale) for s in scores]
        sums.append(functools.reduce(jnp.add, [p.sum(axis=-1, keepdims=True) for p in ps]))
        probs.append([p.astype(BF16) for p in ps])
    outs = []
    for vsl, ps, l in zip(vsls, probs, sums):
        acc = functools.reduce(
            jnp.add, [_dot(p, v_ref[:, vsl]) for p, (_, v_ref) in zip(ps, segments)])
        outs.append(acc / l)
    for p in range(n_heads // 2):
        o_ref[:, vsls[2 * p]] = _pair_select(outs[2 * p], outs[2 * p + 1]).astype(o_ref.dtype)


MLA_PAIRS_PER_STEP = 2


def _mla_sample_attn(q, k_cache, v_cache, k_new, v_new):
    nb, t, _ = q.shape
    past = k_cache.shape[1]
    qk_w = MLA_PAIRS_PER_STEP * 2 * MLA_HEAD_LANES
    v_w = MLA_PAIRS_PER_STEP * 2 * V_DIM
    spec = lambda rows, width: pl.BlockSpec((None, rows, width), lambda b, p: (b, 0, p))
    return pl.pallas_call(
        _mla_sample_kernel,
        grid=(nb, H_A * V_DIM // v_w),
        in_specs=[spec(t, qk_w), spec(past, qk_w), spec(past, v_w), spec(t, qk_w), spec(t, v_w)],
        out_specs=spec(t, v_w),
        out_shape=jax.ShapeDtypeStruct((nb, t, H_A * V_DIM), BF16),
        compiler_params=_params(2),
        name="mla_sample_attn",
    )(q, k_cache, v_cache, k_new, v_new)


def _mla_prompt_kernel(q_ref, k_ref, vt_ref, o_ref, *, tq, tk):
    n_heads = q_ref.shape[1] // MLA_HEAD_LANES
    k_row = lax.broadcasted_iota(jnp.int32, (tk, tq), 0)
    q_col = lax.broadcasted_iota(jnp.int32, (tk, tq), 1)
    v_row = lax.broadcasted_iota(jnp.int32, (2 * V_DIM, tq), 0)
    hsls = [slice(h * MLA_HEAD_LANES, (h + 1) * MLA_HEAD_LANES) for h in range(n_heads)]
    vsls = [slice(h // 2 * 2 * V_DIM, (h // 2 + 1) * 2 * V_DIM) for h in range(n_heads)]
    exp_scale = MLA_SCALE * LOG2_E

    def scores(j, qs):
        start = pl.multiple_of(j * tk, tk)
        return tuple(_dot_nt(k_ref[pl.ds(start, tk), hsl], q) for q, hsl in zip(qs, hsls))

    def attend(j, ss, carry, q_chunk):
        masked = q_chunk is not None
        ps, stats = [], []
        for s, (m, l, _) in zip(ss, carry):
            if masked:
                s = jnp.where(((k_row + j * tk) >> CHUNK_SHIFT) <= q_chunk, s, NEG_INF)
            m_new = jnp.maximum(m, jnp.max(s, axis=0, keepdims=True))
            alpha = jnp.exp2((m - m_new) * exp_scale)
            p = jnp.exp2((s - m_new) * exp_scale)
            ps.append(p.astype(BF16))
            stats.append((m_new, alpha, alpha * l + jnp.sum(p, axis=0, keepdims=True)))
        return tuple(
            (m_new, l, alpha * acc + _dot(vt_ref[j, vsl, :], p))
            for p, vsl, (m_new, alpha, l), (_, _, acc) in zip(ps, vsls, stats, carry))

    def tile(i, _):
        q_start = pl.multiple_of(i * tq, tq)
        qs = [q_ref[pl.ds(q_start, tq), hsl] for hsl in hsls]
        n_full = (i * tq) // tk
        init = tuple((jnp.full((1, tq), NEG_INF, F32), jnp.zeros((1, tq), F32),
                      jnp.zeros((2 * V_DIM, tq), F32)) for _ in range(n_heads))
        carry = lax.fori_loop(0, n_full, lambda j, c: attend(j, scores(j, qs), c, None), init)
        carry = attend(n_full, scores(n_full, qs), carry, (q_col + i * tq) >> CHUNK_SHIFT)
        outs = [acc / l for _, l, acc in carry]
        for p in range(n_heads // 2):
            pair = jnp.where(v_row < V_DIM, outs[2 * p], outs[2 * p + 1])
            o_ref[pl.ds(q_start, tq), vsls[2 * p]] = pair.T.astype(o_ref.dtype)
        return 0

    lax.fori_loop(0, q_ref.shape[0] // tq, tile, 0)


def _mla_prompt_attn(q, k, vt, *, tq, tk):
    nb, sq, _ = q.shape
    pairs = H_A // (2 * MLA_PAIRS_PER_STEP)
    qk_w = MLA_PAIRS_PER_STEP * 2 * MLA_HEAD_LANES
    v_w = MLA_PAIRS_PER_STEP * 2 * V_DIM
    assert vt.shape == (nb, sq // tk, H_A * V_DIM, tk)
    kern = functools.partial(_mla_prompt_kernel, tq=tq, tk=tk)
    return pl.pallas_call(
        kern,
        grid=(nb, pairs),
        in_specs=[pl.BlockSpec((None, sq, qk_w), lambda b, p: (b, 0, p)),
                  pl.BlockSpec((None, sq, qk_w), lambda b, p: (b, 0, p)),
                  pl.BlockSpec((None, sq // tk, v_w, tk), lambda b, p: (b, 0, p, 0))],
        out_specs=pl.BlockSpec((None, sq, v_w), lambda b, p: (b, 0, p)),
        out_shape=jax.ShapeDtypeStruct((nb, sq, H_A * V_DIM), BF16),
        compiler_params=_params(2),
        name="mla_prompt_attn",
    )(q, k, vt)


def _half_masks(x):
    lane = lax.broadcasted_iota(jnp.int32, x.shape, 1)
    zero = jnp.zeros_like(x)
    return jnp.where(lane < HEAD_DIM, x, zero), jnp.where(lane < HEAD_DIM, zero, x)


def _band_core(q_ref, o_ref, rows, segments):
    n_pairs = q_ref.shape[1] // (2 * HEAD_DIM)
    pair_lanes = [slice(p * 2 * HEAD_DIM, (p + 1) * 2 * HEAD_DIM) for p in range(n_pairs)]
    chains = [(2 * p + hh, q, sl) for p, sl in enumerate(pair_lanes)
              for hh, q in enumerate(_half_masks(q_ref[rows, sl]))]
    raw = [[_dot_nt(q, keys(sl)) for keys, _, _, _ in segments] for _, q, sl in chains]
    probs, sums = [], []
    for (h, _, _), head_raw in zip(chains, raw):
        scores = []
        for s, (_, _, bias, live) in zip(head_raw, segments):
            s = s + bias(h)
            scores.append(s if live is None else jnp.where(live, s, NEG_INF))
        m = functools.reduce(jnp.maximum, [s.max(axis=-1, keepdims=True) for s in scores])
        ps = [jnp.exp(s - m) for s in scores]
        sums.append(functools.reduce(jnp.add, [p.sum(axis=-1, keepdims=True) for p in ps]))
        probs.append([p.astype(BF16) for p in ps])
    outs = []
    for (_, _, sl), ps, l in zip(chains, probs, sums):
        acc = functools.reduce(
            jnp.add, [_dot(p, values(sl)) for p, (_, values, _, _) in zip(ps, segments)])
        outs.append(acc / l)
    for p, sl in enumerate(pair_lanes):
        o_ref[rows, sl] = _pair_select(outs[2 * p], outs[2 * p + 1]).astype(o_ref.dtype)


def _band_kernel(q_ref, k_ref, v_ref, bias_ref, o_ref, *, tq, nkb):
    def tile(i, _):
        segments = []
        for d in range(nkb):
            kbi = i - (nkb - 1) + d
            start = pl.multiple_of(jnp.maximum(kbi, 0) * tq, tq)
            segments.append((lambda sl, start=start: k_ref[pl.ds(start, tq), sl],
                             lambda sl, start=start: v_ref[pl.ds(start, tq), sl],
                             lambda h, d=d: bias_ref[h, d], kbi >= 0))
        _band_core(q_ref, o_ref, pl.ds(pl.multiple_of(i * tq, tq), tq), segments)
        return 0

    lax.fori_loop(0, q_ref.shape[0] // tq, tile, 0)


def _band_sample_kernel(q_ref, kc_ref, vc_ref, kn_ref, vn_ref, bc_ref, bn_ref, o_ref):
    _band_core(q_ref, o_ref, slice(None), (
        (lambda sl: kc_ref[:, sl], lambda sl: vc_ref[:, sl], lambda h: bc_ref[h], None),
        (lambda sl: kn_ref[:, sl], lambda sl: vn_ref[:, sl], lambda h: bn_ref[h], None)))


BAND_PAIRS_PER_STEP = 2


def _band_attn(q, k, v, bias, *, tq, nkb):
    nb, sq, w = q.shape
    pw = BAND_PAIRS_PER_STEP * 2 * HEAD_DIM
    heads_per_step = 2 * BAND_PAIRS_PER_STEP
    assert bias.shape == (H_B, nkb, tq, tq) and w % pw == 0 and sq % tq == 0
    kern = functools.partial(_band_kernel, tq=tq, nkb=nkb)
    seq_spec = pl.BlockSpec((None, sq, pw), lambda b, p: (b, 0, p))
    return pl.pallas_call(
        kern,
        grid=(nb, w // pw),
        in_specs=[seq_spec, seq_spec, seq_spec,
                  pl.BlockSpec((heads_per_step, nkb, tq, tq), lambda b, p: (p, 0, 0, 0))],
        out_specs=seq_spec,
        out_shape=jax.ShapeDtypeStruct((nb, sq, w), BF16),
        compiler_params=_params(2),
        name="band_attn",
    )(q, k, v, bias)


def _band_sample_attn(q, k_cache, v_cache, k_new, v_new, bias_cache, bias_new):
    nb, t, w = q.shape
    n_past = k_cache.shape[1]
    pw = BAND_PAIRS_PER_STEP * 2 * HEAD_DIM
    heads_per_step = 2 * BAND_PAIRS_PER_STEP
    assert bias_cache.shape == (H_B, t, n_past) and bias_new.shape == (H_B, t, t)
    spec = lambda rows: pl.BlockSpec((None, rows, pw), lambda b, p: (b, 0, p))
    bias_spec = lambda cols: pl.BlockSpec((heads_per_step, t, cols), lambda b, p: (p, 0, 0))
    return pl.pallas_call(
        _band_sample_kernel,
        grid=(nb, w // pw),
        in_specs=[spec(t), spec(n_past), spec(n_past), spec(t), spec(t),
                  bias_spec(n_past), bias_spec(t)],
        out_specs=spec(t),
        out_shape=jax.ShapeDtypeStruct((nb, t, w), BF16),
        compiler_params=_params(2),
        name="band_sample_attn",
    )(q, k_cache, v_cache, k_new, v_new, bias_cache, bias_new)


def _band_bias_kernel(tab_ref, o_ref, *, tq, tk, nkb, base_off, band_mask, kv_len):
    h = pl.program_id(0)
    d = pl.program_id(1)
    row = lax.broadcasted_iota(jnp.int32, (tq, tk), 0)
    col = lax.broadcasted_iota(jnp.int32, (tq, tk), 1)
    width = _round_up(tq + tk - 1, LANES)
    u = lax.broadcasted_iota(jnp.int32, (SUBLANES, width), 1)
    col_minus_row = jnp.where(u < tk, u, u - width)
    dist = (base_off - d * tk) - col_minus_row
    idx = jnp.clip(dist, -(CHUNK - 1), REL_MAX) + (CHUNK - 1)

    def body(r, acc):
        return jnp.where(idx == r, tab_ref[h, r], acc)

    line = lax.fori_loop(0, N_REL, body, jnp.zeros((SUBLANES, width), F32))
    spread = jnp.broadcast_to(line[0:1, :], (tq, width))
    bias = pltpu.roll(spread, 0, 1, stride=1, stride_axis=0)[:, :tk]
    if band_mask:
        q_chunk = ((nkb - 1) * tk + row) >> CHUNK_SHIFT
        k_chunk = (d * tk + col) >> CHUNK_SHIFT
        gap = q_chunk - k_chunk
        valid = (gap >= 0) & (gap <= PREV_CHUNKS)
    else:
        valid = (d * tk + col) < kv_len
    o_ref[...] = jnp.where(valid, bias, NEG_INF)


def _band_bias(table, *, tq, tk, nkb, base_off, band_mask, kv_len):
    kern = functools.partial(_band_bias_kernel, tq=tq, tk=tk, nkb=nkb, base_off=base_off,
                             band_mask=band_mask, kv_len=kv_len)
    return pl.pallas_call(
        kern,
        grid=(H_B, nkb),
        in_specs=[pl.BlockSpec(memory_space=pltpu.SMEM)],
        out_specs=pl.BlockSpec((None, None, tq, tk), lambda h, d: (h, d, 0, 0)),
        out_shape=jax.ShapeDtypeStruct((H_B, nkb, tq, tk), F32),
        compiler_params=_params(2),
        name="band_bias",
    )(table)


SB_DEAD_TAIL = -110.0
LOG2_E = 1.4426950408889634


def _sb_later2(tk):
    ur = lax.broadcasted_iota(jnp.int32, (tk, tk), 0)
    uc = lax.broadcasted_iota(jnp.int32, (tk, tk), 1)
    later = jnp.where(ur > uc, -1.0, 0.0).astype(BF16)
    return jnp.concatenate([later, later], axis=0)


def _sb_init(tq, n_heads):
    return tuple((jnp.zeros((tq, 1), F32), jnp.zeros((tq, 2 * HEAD_DIM), F32))
                 for _ in range(n_heads))


def _sb_blocks(qs, blocks, carry):
    zs = [[_dot_nt(q, kb) for q, kb in zip(qs, kbs)] for kbs, _, _, _ in blocks]
    sps, log_sigs = [], []
    for block_zs, (_, _, _, valid) in zip(zs, blocks):
        block_sps, block_log_sigs = [], []
        for z in block_zs:
            t = jnp.log(1.0 + jnp.exp2(jnp.abs(z) * -LOG2_E))
            sp = jnp.maximum(z, 0.0) + t
            block_log_sigs.append(z - sp)
            block_sps.append(sp if valid is None else jnp.where(valid, sp, 0.0))
        sps.append(block_sps)
        log_sigs.append(block_log_sigs)
    tails = []
    for b, (block_sps, (_, _, later2, _)) in enumerate(zip(sps, blocks)):
        block_tails = []
        for sp, (tail0, _) in zip(block_sps, carry):
            sp_hi = sp.astype(BF16)
            sp_lo = (sp - sp_hi.astype(F32)).astype(BF16)
            inner = _dot(jnp.concatenate([sp_hi, sp_lo], axis=1), later2)
            block_tails.append(inner + tail0 if b == 0 else inner)
        tails.append(block_tails)
    out = []
    for h, (tail0, acc) in enumerate(carry):
        for b, (_, vbs, _, valid) in enumerate(blocks):
            tail = tails[b][h] if b == 0 else tails[b][h] + tail0
            w = jnp.exp(log_sigs[b][h] + tail)
            if valid is not None:
                w = jnp.where(valid, w, 0.0)
            acc = acc + _dot(w.astype(BF16), vbs[h])
            tail0 = tail0 - jnp.sum(sps[b][h], axis=-1, keepdims=True)
        out.append((tail0, acc))
    return tuple(out)


def _sb_block(qs, kbs, vbs, carry, later2, valid):
    return _sb_blocks(qs, [(kbs, vbs, later2, valid)], carry)


def _sb_sweep(block, n_blocks, carry):
    def alive(c):
        return functools.reduce(jnp.maximum, [jnp.max(t0) for t0, _ in c]) > SB_DEAD_TAIL

    def cond(state):
        j, live, _ = state
        return (j >= 0) & live

    def body(state):
        j, _, c = state
        c = block(j, c)
        return j - 1, alive(c), c

    return lax.while_loop(cond, body, (n_blocks - 1, alive(carry), carry))[2]


def _sb_kernel(q_ref, k_ref, v_ref, o_ref, *, tq):
    sq = q_ref.shape[0]
    n_pairs = q_ref.shape[1] // (2 * HEAD_DIM)
    pair_lanes = [slice(p * 2 * HEAD_DIM, (p + 1) * 2 * HEAD_DIM) for p in range(n_pairs)]
    row = lax.broadcasted_iota(jnp.int32, (tq, tq), 0)
    col = lax.broadcasted_iota(jnp.int32, (tq, tq), 1)
    below_diagonal = col < row
    later2 = _sb_later2(tq)

    def tile(i, first):
        q_start = pl.multiple_of(i * tq, tq)
        chains = [(q, sl) for sl in pair_lanes
                  for q in _half_masks(q_ref[pl.ds(q_start, tq), sl])]
        qs = [q for q, _ in chains]

        def block(j, valid):
            start = pl.multiple_of(j * tq, tq)
            return ([k_ref[pl.ds(start, tq), sl] for _, sl in chains],
                    [v_ref[pl.ds(start, tq), sl] for _, sl in chains], later2, valid)

        init = _sb_init(tq, len(chains))
        if first:
            carry = _sb_blocks(qs, [block(i, below_diagonal)], init)
        else:
            carry = _sb_blocks(qs, [block(i, below_diagonal), block(i - 1, None)], init)
            carry = _sb_sweep(lambda j, c: _sb_blocks(qs, [block(j, None)], c), i - 1, carry)
        for p, sl in enumerate(pair_lanes):
            o_ref[pl.ds(q_start, tq), sl] = _pair_select(
                carry[2 * p][1], carry[2 * p + 1][1]).astype(o_ref.dtype)

    tile(0, True)

    def later_tile(i, _):
        tile(i, False)
        return 0

    lax.fori_loop(1, sq // tq, later_tile, 0)


SB_PAIRS_PER_STEP = 2


def _sb_attn(q, k, v, *, tq):
    nb, sq, w = q.shape
    sk = k.shape[1]
    pw = SB_PAIRS_PER_STEP * 2 * HEAD_DIM
    assert sk == sq and sq % tq == 0 and w % pw == 0
    kern = functools.partial(_sb_kernel, tq=tq)
    return pl.pallas_call(
        kern,
        grid=(nb, w // pw),
        in_specs=[pl.BlockSpec((None, sq, pw), lambda b, p: (b, 0, p)),
                  pl.BlockSpec((None, sk, pw), lambda b, p: (b, 0, p)),
                  pl.BlockSpec((None, sk, pw), lambda b, p: (b, 0, p))],
        out_specs=pl.BlockSpec((None, sq, pw), lambda b, p: (b, 0, p)),
        out_shape=jax.ShapeDtypeStruct((nb, sq, w), BF16),
        compiler_params=_params(2),
        name="sb_attn",
    )(q, k, v)


SB_CACHE_BLOCK = 256


def _sb_sample_kernel(q_ref, kc_ref, vc_ref, kn_ref, vn_ref, o_ref, *, tc):
    t = q_ref.shape[0]
    past = kc_ref.shape[0]
    n_pairs = q_ref.shape[1] // (2 * HEAD_DIM)
    pair_lanes = [slice(p * 2 * HEAD_DIM, (p + 1) * 2 * HEAD_DIM) for p in range(n_pairs)]
    qs = [q for sl in pair_lanes for q in _half_masks(q_ref[:, sl])]
    per_head = lambda blocks: [b for b in blocks for _ in range(2)]
    row = lax.broadcasted_iota(jnp.int32, (t, t), 0)
    col = lax.broadcasted_iota(jnp.int32, (t, t), 1)
    carry = _sb_block(qs, per_head([kn_ref[:, sl] for sl in pair_lanes]),
                      per_head([vn_ref[:, sl] for sl in pair_lanes]),
                      _sb_init(t, len(qs)), _sb_later2(t), col < row)
    later2 = _sb_later2(tc)

    def block(j, c):
        start = pl.multiple_of(j * tc, tc)
        kbs = [kc_ref[pl.ds(start, tc), sl] for sl in pair_lanes]
        vbs = [vc_ref[pl.ds(start, tc), sl] for sl in pair_lanes]
        return _sb_block(qs, per_head(kbs), per_head(vbs), c, later2, None)

    carry = _sb_sweep(block, past // tc, carry)
    for p, sl in enumerate(pair_lanes):
        o_ref[:, sl] = _pair_select(carry[2 * p][1], carry[2 * p + 1][1]).astype(o_ref.dtype)


def _sb_sample_attn(q, k_cache, v_cache, k_new, v_new):
    nb, t, w = q.shape
    past = k_cache.shape[1]
    pw = SB_PAIRS_PER_STEP * 2 * HEAD_DIM
    tc = min(SB_CACHE_BLOCK, past)
    assert past % tc == 0 and w % pw == 0
    kern = functools.partial(_sb_sample_kernel, tc=tc)
    new_spec = pl.BlockSpec((None, t, pw), lambda b, p: (b, 0, p))
    cache_spec = pl.BlockSpec((None, past, pw), lambda b, p: (b, 0, p))
    return pl.pallas_call(
        kern,
        grid=(nb, w // pw),
        in_specs=[new_spec, cache_spec, cache_spec, new_spec, new_spec],
        out_specs=new_spec,
        out_shape=jax.ShapeDtypeStruct((nb, t, w), BF16),
        compiler_params=_params(2),
        name="sb_sample_attn",
    )(q, k_cache, v_cache, k_new, v_new)


HC = H_C * HEAD_DIM


def _odd_proj_kernel(x_ref, w_ref, q_ref, k32_ref, v32_ref, k16_ref, v16_ref):
    xb = x_ref[...].astype(BF16)
    h = _dot(xb, w_ref[...])
    q_ref[...] = (h[:, 0:HC] * QK_SCALE).astype(BF16)
    k = h[:, HC:2 * HC]
    v = h[:, 2 * HC:3 * HC]
    k32_ref[...] = k
    v32_ref[...] = v
    k16_ref[...] = k.astype(BF16)
    v16_ref[...] = v.astype(BF16)


def _odd_proj(x, w):
    t, d = x.shape
    tm = min(ROW_TILE, t)
    return pl.pallas_call(
        _odd_proj_kernel,
        grid=(t // tm,),
        in_specs=[_rows(tm, d), _resident(w.shape)],
        out_specs=(_rows(tm, HC),) * 5,
        out_shape=(jax.ShapeDtypeStruct((t, HC), BF16), jax.ShapeDtypeStruct((t, HC), F32),
                   jax.ShapeDtypeStruct((t, HC), F32), jax.ShapeDtypeStruct((t, HC), BF16),
                   jax.ShapeDtypeStruct((t, HC), BF16)),
        compiler_params=_params(1, V7X_VMEM_LIMIT_BYTES),
        name="odd_in_proj",
    )(x, w)


def _post_block_kernel(*refs, n_mix, alpha):
    x_ref = refs[0]
    mix_refs = refs[1:1 + n_mix]
    w_refs = refs[1 + n_mix:1 + 2 * n_mix]
    g1_ref, b1_ref, wu_ref, wd_ref, g2_ref, b2_ref, o_ref = refs[1 + 2 * n_mix:]
    y = _dot(mix_refs[0][...], w_refs[0][...])
    for m_ref, w_ref in zip(mix_refs[1:], w_refs[1:]):
        y = y + _dot(m_ref[...], w_ref[...])
    x = _layer_norm(alpha * x_ref[...] + y, g1_ref[...], b1_ref[...])
    h = _dot(x.astype(BF16), wu_ref[...])
    h = jnp.maximum(h, 0.0)
    ff = _dot((h * h).astype(BF16), wd_ref[...])
    o_ref[...] = _layer_norm(alpha * x + ff, g2_ref[...], b2_ref[...])


def _post_block(x, mixes, ws, g1, b1, wu, wd, g2, b2, alpha):
    t, d = x.shape
    tm = min(ROW_TILE, t)
    kern = functools.partial(_post_block_kernel, n_mix=len(mixes), alpha=alpha)
    consts = list(ws) + [g1, b1, wu, wd, g2, b2]
    return pl.pallas_call(
        kern,
        grid=(t // tm,),
        in_specs=([_rows(tm, d)] + [_rows(tm, m.shape[1]) for m in mixes]
                  + [_resident(c.shape) for c in consts]),
        out_specs=_rows(tm, d),
        out_shape=jax.ShapeDtypeStruct((t, d), F32),
        compiler_params=_params(1, V7X_VMEM_LIMIT_BYTES),
        name="post_block",
    )(x, *mixes, *consts)


def _rope_tables(pos, reps):
    inv = jnp.power(ROPE_THETA, -jnp.arange(ROPE_HALF, dtype=F32) / ROPE_HALF)
    ang = pos.astype(F32)[:, None] * inv[None, :]
    cos, sin = jnp.cos(ang), jnp.sin(ang)
    n = pos.shape[0]
    z = lambda w: jnp.zeros((n, w), F32)
    tail = LANES - ROPE_LANE0 - ROPE_DIM
    c = jnp.concatenate([jnp.ones((n, ROPE_LANE0), F32), cos, cos, z(tail)], axis=1)
    s1 = jnp.concatenate([z(ROPE_LANE0), -sin, z(ROPE_HALF), z(tail)], axis=1)
    s2 = jnp.concatenate([z(ROPE_LANE0), z(ROPE_HALF), sin, z(tail)], axis=1)
    return tuple(jnp.tile(a, (reps, 1)) for a in (c, s1, s2))


CAST_ROWS = 256


def _cast_kernel(w_ref, o_ref):
    o_ref[...] = w_ref[...].astype(o_ref.dtype)


def _to_bf16(w):
    r, c = w.shape
    return pl.pallas_call(
        _cast_kernel,
        grid=(r // CAST_ROWS,),
        in_specs=[_rows(CAST_ROWS, c)],
        out_specs=_rows(CAST_ROWS, c),
        out_shape=jax.ShapeDtypeStruct((r, c), BF16),
        compiler_params=_params(1),
        name="weight_cast",
    )(w)


def _even_weights(w_in, g_q, w_uq, g_kv, w_ukv, w_out):
    d = w_in.shape[0]
    o_r = Q_LORA + KV_LORA
    tail = LANES - ROPE_LANE0 - ROPE_DIM
    w_in = _to_bf16(w_in)
    w1 = jnp.concatenate(
        [w_in[:, :o_r], jnp.zeros((d, ROPE_LANE0), BF16), w_in[:, o_r:o_r + ROPE_DIM],
         jnp.zeros((d, tail), BF16), w_in[:, o_r + ROPE_DIM:]], axis=1)
    per_head = NOPE_DIM + ROPE_DIM
    wuq = jnp.pad(_to_bf16(w_uq).reshape(Q_LORA, H_A, per_head),
                  ((0, 0), (0, 0), (0, MLA_HEAD_LANES - per_head)))
    wuq = wuq.reshape(Q_LORA, H_A * MLA_HEAD_LANES)
    wkv = _to_bf16(w_ukv).reshape(KV_LORA, H_A, NOPE_DIM + V_DIM)
    wk = jnp.pad(wkv[:, :, :NOPE_DIM], ((0, 0), (0, 0), (0, MLA_HEAD_LANES - NOPE_DIM)))
    wk = wk.reshape(KV_LORA, H_A * MLA_HEAD_LANES)
    wv = wkv[:, :, NOPE_DIM:].reshape(KV_LORA, H_A * V_DIM)
    wo = _to_bf16(w_out)
    wo_a, wo_b = wo[:H_A * V_DIM], wo[H_A * V_DIM:]
    return dict(w1=w1, gq=g_q[None, :], gkv=g_kv[None, :], wuq=wuq, wk=wk, wv=wv,
                wo_a=wo_a, wo_b=wo_b)


def _round_up(n, m):
    return (n + m - 1) // m * m


ATTN_TILE = 256


def _even_mixer(x, nb, t, ropes, w, table, cache):
    qpad, ckv, krw, qb, kb32, vb32, kb16, vb16 = _even_proj(x, w["w1"], w["gq"], w["gkv"], w["wuq"], ropes)
    r3 = lambda a: a.reshape(nb, t, a.shape[-1])
    qpad, qb, kb16, vb16 = map(r3, (qpad, qb, kb16, vb16))
    kb32 = kb32.reshape(nb, t, H_B, HEAD_DIM)
    vb32 = vb32.reshape(nb, t, H_B, HEAD_DIM)
    if cache is None:
        tq = min(ATTN_TILE, t)
        tk = min(2 * tq, t)
        kpad, vt = _kv_up_blocked(ckv, krw, w["wk"], w["wv"].T, nb, t, tk)
        o_a = _mla_prompt_attn(qpad, r3(kpad), vt, tq=tq, tk=tk)
        nkb = min(BAND_PAST // tq + 1, t // tq)
        bias = _band_bias(table, tq=tq, tk=tq, nkb=nkb, base_off=(nkb - 1) * tq,
                          band_mask=True, kv_len=t)
        o_b = _band_attn(qb, kb16, vb16, bias, tq=tq, nkb=nkb)
        rows = min(BAND_PAST, t)
        new_bk, new_bv = kb32[:, t - rows:], vb32[:, t - rows:]
    else:
        c_ckv, c_kr, c_bk, c_bv = cache
        past = c_ckv.shape[1]
        tail = LANES - ROPE_LANE0 - ROPE_DIM
        c_krw = jnp.pad(c_kr, ((0, 0), (0, 0), (ROPE_LANE0, tail)))
        kpad_c, v16_c = _kv_up(c_ckv.reshape(nb * past, KV_LORA), c_krw.reshape(nb * past, LANES),
                               w["wk"], w["wv"])
        kpad, v16 = _kv_up(ckv, krw, w["wk"], w["wv"])
        assert past % CHUNK == 0 and t <= CHUNK
        o_a = _mla_sample_attn(qpad, kpad_c.reshape(nb, past, -1), v16_c.reshape(nb, past, -1),
                               r3(kpad), r3(v16))
        n_past = c_bk.shape[1]
        band_len = n_past + t
        bias = _band_bias(table, tq=t, tk=_round_up(band_len, LANES), nkb=1, base_off=n_past,
                          band_mask=False, kv_len=band_len)[:, 0]
        flat16 = lambda c: c.reshape(nb, n_past, HB).astype(BF16)
        o_b = _band_sample_attn(qb, flat16(c_bk), flat16(c_bv), kb16, vb16,
                                bias[:, :, :n_past], bias[:, :, n_past:band_len])
        rows = min(BAND_PAST, band_len)
        new_bk = jnp.concatenate([c_bk, kb32], axis=1)[:, band_len - rows:]
        new_bv = jnp.concatenate([c_bv, vb32], axis=1)[:, band_len - rows:]
    mixes = (o_a.reshape(nb * t, -1), o_b.reshape(nb * t, -1))
    new_ckv = ckv.reshape(nb, t, KV_LORA)
    new_kr = krw[:, ROPE_LANE0:ROPE_LANE0 + ROPE_DIM].reshape(nb, t, ROPE_DIM)
    return mixes, (w["wo_a"], w["wo_b"]), (new_ckv, new_kr, new_bk, new_bv)


def _odd_mixer(x, nb, t, w_in, w_out, cache):
    q16, k32, v32, k16, v16 = _odd_proj(x, w_in)
    r3 = lambda a: a.reshape(nb, t, HC)
    q16, k16, v16 = map(r3, (q16, k16, v16))
    if cache is None:
        tq = min(ATTN_TILE, t)
        o = _sb_attn(q16, k16, v16, tq=tq)
    else:
        c_k, c_v = cache
        past = c_k.shape[1]
        flat16 = lambda c: c.reshape(nb, past, HC).astype(BF16)
        o = _sb_sample_attn(q16, flat16(c_k), flat16(c_v), k16, v16)
    new_k = k32.reshape(nb, t, H_C, HEAD_DIM)
    new_v = v32.reshape(nb, t, H_C, HEAD_DIM)
    return (o.reshape(nb * t, HC),), (w_out,), (new_k, new_v)


def kernel(x_prompt, x_sample, cache_mla_ckv, cache_mla_krope, cache_band_k, cache_band_v,
           cache_sb_k, cache_sb_v, w_in_ab, g_q_lat, w_uq, g_kv_lat, w_ukv, rel_bias, w_out_ab,
           w_in_c, w_out_c, ln_mix_g, ln_mix_b, ln_ffn_g, ln_ffn_b, w_ff_up, w_ff_down):
    nb_p, t_p, d = x_prompt.shape
    nb_s, t_s, _ = x_sample.shape
    past = cache_mla_ckv.shape[2]
    depth = ln_mix_g.shape[0]
    alpha = (2.0 * depth) ** 0.25
    xp = x_prompt.reshape(nb_p * t_p, d)
    xs = x_sample.reshape(nb_s * t_s, d)
    ropes_p = _rope_tables(jnp.arange(t_p, dtype=jnp.int32), max(1, ROW_TILE // t_p))
    ropes_s = _rope_tables(past + jnp.arange(t_s, dtype=jnp.int32), max(1, ROW_TILE // t_s))
    even_p, even_s, odd_p, odd_s = [], [], [], []
    for l in range(depth):
        i = l // 2
        if l % 2 == 0:
            w = _even_weights(w_in_ab[i], g_q_lat[i], w_uq[i], g_kv_lat[i], w_ukv[i], w_out_ab[i])
            mp, wo, new_p = _even_mixer(xp, nb_p, t_p, ropes_p, w, rel_bias[i], None)
            ms, _, new_s = _even_mixer(
                xs, nb_s, t_s, ropes_s, w, rel_bias[i],
                (cache_mla_ckv[i], cache_mla_krope[i], cache_band_k[i], cache_band_v[i]))
            even_p.append(new_p)
            even_s.append(new_s)
        else:
            w_in = _to_bf16(w_in_c[i])
            w_out = _to_bf16(w_out_c[i])
            mp, wo, new_p = _odd_mixer(xp, nb_p, t_p, w_in, w_out, None)
            ms, _, new_s = _odd_mixer(xs, nb_s, t_s, w_in, w_out, (cache_sb_k[i], cache_sb_v[i]))
            odd_p.append(new_p)
            odd_s.append(new_s)
        g1, b1 = ln_mix_g[l][None, :], ln_mix_b[l][None, :]
        g2, b2 = ln_ffn_g[l][None, :], ln_ffn_b[l][None, :]
        wu, wd = _to_bf16(w_ff_up[l]), _to_bf16(w_ff_down[l])
        xp = _post_block(xp, mp, wo, g1, b1, wu, wd, g2, b2, alpha)
        xs = _post_block(xs, ms, wo, g1, b1, wu, wd, g2, b2, alpha)
    stack = lambda groups, k: jnp.stack([g[k] for g in groups])
    return (xp.reshape(nb_p, t_p, d), xs.reshape(nb_s, t_s, d),
            stack(even_p, 0), stack(even_p, 1), stack(even_p, 2), stack(even_p, 3),
            stack(odd_p, 0), stack(odd_p, 1),
            stack(even_s, 0), stack(even_s, 1), stack(even_s, 2), stack(even_s, 3),
            stack(odd_s, 0), stack(odd_s, 1))
```

```python
import functools

import jax
import jax.numpy as jnp
from jax import lax
from jax.experimental import pallas as pl
from jax.experimental.pallas import tpu as pltpu

F32 = jnp.float32
BF16 = jnp.bfloat16

CHUNK = 64
CHUNK_SHIFT = 6
HEAD_DIM = 64
H_A = 8
Q_LORA = 768
KV_LORA = 256
NOPE_DIM = 64
ROPE_DIM = 32
V_DIM = 64
ROPE_THETA = 10000.0
MLA_SCALE = (NOPE_DIM + ROPE_DIM) ** -0.5
H_B = 8
PREV_CHUNKS = 8
BAND_PAST = PREV_CHUNKS * CHUNK
REL_MAX = 256
N_REL = REL_MAX + CHUNK
H_C = 16
QK_SCALE = HEAD_DIM ** -0.5
NEG_INF = -1e30

LANES = 128
SUBLANES = 8
V7X_VMEM_LIMIT_BYTES = 56 * 1024 * 1024

MLA_HEAD_LANES = LANES
ROPE_LANE0 = NOPE_DIM
ROPE_HALF = ROPE_DIM // 2

ROW_TILE = 512


def _params(n_axes, vmem_bytes=None):
    return pltpu.CompilerParams(
        dimension_semantics=("parallel",) * n_axes,
        vmem_limit_bytes=vmem_bytes,
    )


def _resident(shape):
    nd = len(shape)
    return pl.BlockSpec(shape, lambda *_: (0,) * nd, pipeline_mode=pl.Buffered(1))


def _rows(tm, width):
    return pl.BlockSpec((tm, width), lambda i: (i, 0))


def _dot(a, b):
    return jnp.dot(a, b, preferred_element_type=F32)


def _dot_nt(a, b):
    return lax.dot_general(a, b, (((1,), (1,)), ((), ())), preferred_element_type=F32)


def _rms(h, g, eps=1e-6):
    return h * lax.rsqrt(jnp.mean(h * h, axis=-1, keepdims=True) + eps) * g


def _layer_norm(r, g, b, eps=1e-5):
    mu = jnp.mean(r, axis=-1, keepdims=True)
    d = r - mu
    var = jnp.mean(d * d, axis=-1, keepdims=True)
    return d * lax.rsqrt(var + eps) * g + b


N_BAND = 3 * H_B * HEAD_DIM
W1_Q0, W1_Q1 = 0, Q_LORA
W1_C0, W1_C1 = W1_Q1, W1_Q1 + KV_LORA
W1_R0, W1_R1 = W1_C1, W1_C1 + LANES
W1_B0, W1_B1 = W1_R1, W1_R1 + N_BAND
HB = H_B * HEAD_DIM


def _even_proj_kernel(x_ref, w1_ref, gq_ref, gkv_ref, wuq_ref, c_ref, s1_ref, s2_ref,
                      qpad_ref, ckv_ref, krw_ref, qb_ref, kb32_ref, vb32_ref, kb16_ref, vb16_ref):
    xb = x_ref[...].astype(BF16)
    c = c_ref[...]
    s1 = s1_ref[...]
    s2 = s2_ref[...]

    def rope(v):
        return (v * c + pltpu.roll(v, LANES - ROPE_HALF, 1) * s1
                + pltpu.roll(v, ROPE_HALF, 1) * s2)

    hq = _dot(xb, w1_ref[:, W1_Q0:W1_Q1])
    qn = _rms(hq, gq_ref[...]).astype(BF16)
    qa = _dot(qn, wuq_ref[...])
    for h in range(H_A):
        sl = slice(h * MLA_HEAD_LANES, (h + 1) * MLA_HEAD_LANES)
        qpad_ref[:, sl] = rope(qa[:, sl]).astype(BF16)

    hc = _dot(xb, w1_ref[:, W1_C0:W1_C1])
    ckv_ref[...] = _rms(hc, gkv_ref[...])

    hk = _dot(xb, w1_ref[:, W1_R0:W1_R1])
    krw_ref[...] = rope(hk)

    hb = _dot(xb, w1_ref[:, W1_B0:W1_B1])
    qb_ref[...] = (hb[:, 0:HB] * QK_SCALE).astype(BF16)
    kb = hb[:, HB:2 * HB]
    vb = hb[:, 2 * HB:3 * HB]
    kb32_ref[...] = kb
    vb32_ref[...] = vb
    kb16_ref[...] = kb.astype(BF16)
    vb16_ref[...] = vb.astype(BF16)


def _even_proj(x, w1, gq, gkv, wuq, rope_tabs):
    t, d = x.shape
    tm = min(ROW_TILE, t)
    c, s1, s2 = rope_tabs
    period_tiles = c.shape[0] // tm
    assert c.shape[0] % tm == 0
    rope_spec = pl.BlockSpec((tm, LANES), lambda i: (i % period_tiles, 0))
    qw = H_A * MLA_HEAD_LANES
    out_shape = (
        jax.ShapeDtypeStruct((t, qw), BF16),
        jax.ShapeDtypeStruct((t, KV_LORA), F32),
        jax.ShapeDtypeStruct((t, LANES), F32),
        jax.ShapeDtypeStruct((t, HB), BF16),
        jax.ShapeDtypeStruct((t, HB), F32),
        jax.ShapeDtypeStruct((t, HB), F32),
        jax.ShapeDtypeStruct((t, HB), BF16),
        jax.ShapeDtypeStruct((t, HB), BF16),
    )
    return pl.pallas_call(
        _even_proj_kernel,
        grid=(t // tm,),
        in_specs=[_rows(tm, d), _resident(w1.shape), _resident(gq.shape), _resident(gkv.shape),
                  _resident(wuq.shape), rope_spec, rope_spec, rope_spec],
        out_specs=(_rows(tm, qw), _rows(tm, KV_LORA), _rows(tm, LANES), _rows(tm, HB),
                   _rows(tm, HB), _rows(tm, HB), _rows(tm, HB), _rows(tm, HB)),
        out_shape=out_shape,
        compiler_params=_params(1, V7X_VMEM_LIMIT_BYTES),
        name="even_in_proj",
    )(x, w1, gq, gkv, wuq, c, s1, s2)


def _kv_up_kernel(ckv_ref, krw_ref, wk_ref, wv_ref, kpad_ref, v_ref, *, values_on_rows):
    cb = ckv_ref[...].astype(BF16)
    krw = krw_ref[...]
    k = _dot(cb, wk_ref[...])
    for h in range(H_A):
        sl = slice(h * MLA_HEAD_LANES, (h + 1) * MLA_HEAD_LANES)
        kpad_ref[:, sl] = (k[:, sl] + krw).astype(BF16)
    if values_on_rows:
        v_ref[...] = _dot_nt(wv_ref[...], cb).astype(BF16)
    else:
        v_ref[...] = _dot(cb, wv_ref[...]).astype(BF16)


def _kv_up(ckv, krw, wk, wv):
    t = ckv.shape[0]
    tm = min(ROW_TILE, t)
    kw = H_A * MLA_HEAD_LANES
    vw = H_A * V_DIM
    return pl.pallas_call(
        functools.partial(_kv_up_kernel, values_on_rows=False),
        grid=(t // tm,),
        in_specs=[_rows(tm, KV_LORA), _rows(tm, LANES), _resident(wk.shape), _resident(wv.shape)],
        out_specs=(_rows(tm, kw), _rows(tm, vw)),
        out_shape=(jax.ShapeDtypeStruct((t, kw), BF16), jax.ShapeDtypeStruct((t, vw), BF16)),
        compiler_params=_params(1),
        name="mla_kv_up",
    )(ckv, krw, wk, wv)


def _kv_up_blocked(ckv, krw, wk, wv_t, nb, seq, tk):
    t = ckv.shape[0]
    kw = H_A * MLA_HEAD_LANES
    vw = H_A * V_DIM
    per_seq = seq // tk
    return pl.pallas_call(
        functools.partial(_kv_up_kernel, values_on_rows=True),
        grid=(t // tk,),
        in_specs=[_rows(tk, KV_LORA), _rows(tk, LANES), _resident(wk.shape), _resident(wv_t.shape)],
        out_specs=(_rows(tk, kw),
                   pl.BlockSpec((None, None, vw, tk), lambda i: (i // per_seq, i % per_seq, 0, 0))),
        out_shape=(jax.ShapeDtypeStruct((t, kw), BF16),
                   jax.ShapeDtypeStruct((nb, per_seq, vw, tk), BF16)),
        compiler_params=_params(1),
        name="mla_kv_up_blocked",
    )(ckv, krw, wk, wv_t)


def _pair_select(lo, hi):
    lane = lax.broadcasted_iota(jnp.int32, lo.shape, 1)
    return jnp.where(lane < HEAD_DIM, lo, hi)


def _mla_sample_kernel(q_ref, kc_ref, vc_ref, kn_ref, vn_ref, o_ref):
    n_heads = q_ref.shape[1] // MLA_HEAD_LANES
    hsls = [slice(h * MLA_HEAD_LANES, (h + 1) * MLA_HEAD_LANES) for h in range(n_heads)]
    vsls = [slice(h // 2 * 2 * V_DIM, (h // 2 + 1) * 2 * V_DIM) for h in range(n_heads)]
    segments = ((kc_ref, vc_ref), (kn_ref, vn_ref))
    exp_scale = MLA_SCALE * LOG2_E
    raw = [[_dot_nt(q_ref[:, hsl], k_ref[:, hsl]) for k_ref, _ in segments] for hsl in hsls]
    probs, sums = [], []
    for scores in raw:
        m = functools.reduce(jnp.maximum, [s.max(axis=-1, keepdims=True) for s in scores])
        ps = [jnp.exp2((s - m) * exp_scale) for s in scores]
        sums.append(functools.reduce(jnp.add, [p.sum(axis=-1, keepdims=True) for p in ps]))
        probs.append([p.astype(BF16) for p in ps])
    outs = []
    for vsl, ps, l in zip(vsls, probs, sums):
        acc = functools.reduce(
            jnp.add, [_dot(p, v_ref[:, vsl]) for p, (_, v_ref) in zip(ps, segments)])
        outs.append(acc / l)
    for p in range(n_heads // 2):
        o_ref[:, vsls[2 * p]] = _pair_select(outs[2 * p], outs[2 * p + 1]).astype(o_ref.dtype)


MLA_PAIRS_PER_STEP = 2


def _mla_sample_attn(q, k_cache, v_cache, k_new, v_new):
    nb, t, _ = q.shape
    past = k_cache.shape[1]
    qk_w = MLA_PAIRS_PER_STEP * 2 * MLA_HEAD_LANES
    v_w = MLA_PAIRS_PER_STEP * 2 * V_DIM
    spec = lambda rows, width: pl.BlockSpec((None, rows, width), lambda b, p: (b, 0, p))
    return pl.pallas_call(
        _mla_sample_kernel,
        grid=(nb, H_A * V_DIM // v_w),
        in_specs=[spec(t, qk_w), spec(past, qk_w), spec(past, v_w), spec(t, qk_w), spec(t, v_w)],
        out_specs=spec(t, v_w),
        out_shape=jax.ShapeDtypeStruct((nb, t, H_A * V_DIM), BF16),
        compiler_params=_params(2),
        name="mla_sample_attn",
    )(q, k_cache, v_cache, k_new, v_new)


def _mla_prompt_kernel(q_ref, k_ref, vt_ref, o_ref, *, tq, tk):
    n_heads = q_ref.shape[1] // MLA_HEAD_LANES
    tiles_per_block = tk // tq
    v_row = lax.broadcasted_iota(jnp.int32, (2 * V_DIM, tq), 0)
    hsls = [slice(h * MLA_HEAD_LANES, (h + 1) * MLA_HEAD_LANES) for h in range(n_heads)]
    vsls = [slice(h // 2 * 2 * V_DIM, (h // 2 + 1) * 2 * V_DIM) for h in range(n_heads)]
    exp_scale = MLA_SCALE * LOG2_E

    def attend(qs, carry, block, span, visible):
        start = pl.multiple_of(block * tk, tk)
        ss = [_dot_nt(k_ref[pl.ds(start, span), hsl], q) for q, hsl in zip(qs, hsls)]
        ps, stats = [], []
        for s, (m, l, _) in zip(ss, carry):
            if visible is not None:
                s = jnp.where(visible, s, NEG_INF)
            m_new = jnp.maximum(m, jnp.max(s, axis=0, keepdims=True))
            alpha = jnp.exp2((m - m_new) * exp_scale)
            p = jnp.exp2((s - m_new) * exp_scale)
            ps.append(p.astype(BF16))
            stats.append((m_new, alpha, alpha * l + jnp.sum(p, axis=0, keepdims=True)))
        return tuple(
            (m_new, l, alpha * acc + _dot(vt_ref[block, vsl, 0:span], p))
            for p, vsl, (m_new, alpha, l), (_, _, acc) in zip(ps, vsls, stats, carry))

    def tile(block, g):
        i = block * tiles_per_block + g
        q_start = pl.multiple_of(i * tq, tq)
        qs = [q_ref[pl.ds(q_start, tq), hsl] for hsl in hsls]
        init = tuple((jnp.full((1, tq), NEG_INF, F32), jnp.zeros((1, tq), F32),
                      jnp.zeros((2 * V_DIM, tq), F32)) for _ in range(n_heads))
        carry = lax.fori_loop(0, block, lambda j, c: attend(qs, c, j, tk, None), init)
        span = (g + 1) * tq
        k_chunk = (lax.broadcasted_iota(jnp.int32, (span, tq), 0) + block * tk) >> CHUNK_SHIFT
        q_chunk = (lax.broadcasted_iota(jnp.int32, (span, tq), 1) + i * tq) >> CHUNK_SHIFT
        carry = attend(qs, carry, block, span, k_chunk <= q_chunk)
        outs = [acc / l for _, l, acc in carry]
        for p in range(n_heads // 2):
            pair = jnp.where(v_row < V_DIM, outs[2 * p], outs[2 * p + 1])
            o_ref[pl.ds(q_start, tq), vsls[2 * p]] = pair.T.astype(o_ref.dtype)

    def block_tiles(block, _):
        for g in range(tiles_per_block):
            tile(block, g)
        return 0

    lax.fori_loop(0, q_ref.shape[0] // tk, block_tiles, 0)


def _mla_prompt_attn(q, k, vt, *, tq, tk):
    nb, sq, _ = q.shape
    pairs = H_A // (2 * MLA_PAIRS_PER_STEP)
    qk_w = MLA_PAIRS_PER_STEP * 2 * MLA_HEAD_LANES
    v_w = MLA_PAIRS_PER_STEP * 2 * V_DIM
    assert vt.shape == (nb, sq // tk, H_A * V_DIM, tk)
    kern = functools.partial(_mla_prompt_kernel, tq=tq, tk=tk)
    return pl.pallas_call(
        kern,
        grid=(nb, pairs),
        in_specs=[pl.BlockSpec((None, sq, qk_w), lambda b, p: (b, 0, p)),
                  pl.BlockSpec((None, sq, qk_w), lambda b, p: (b, 0, p)),
                  pl.BlockSpec((None, sq // tk, v_w, tk), lambda b, p: (b, 0, p, 0))],
        out_specs=pl.BlockSpec((None, sq, v_w), lambda b, p: (b, 0, p)),
        out_shape=jax.ShapeDtypeStruct((nb, sq, H_A * V_DIM), BF16),
        compiler_params=_params(2),
        name="mla_prompt_attn",
    )(q, k, vt)


def _half_masks(x):
    lane = lax.broadcasted_iota(jnp.int32, x.shape, 1)
    zero = jnp.zeros_like(x)
    return jnp.where(lane < HEAD_DIM, x, zero), jnp.where(lane < HEAD_DIM, zero, x)


def _band_core(q_ref, o_ref, rows, segments):
    n_pairs = q_ref.shape[1] // (2 * HEAD_DIM)
    pair_lanes = [slice(p * 2 * HEAD_DIM, (p + 1) * 2 * HEAD_DIM) for p in range(n_pairs)]
    chains = [(2 * p + hh, q, sl) for p, sl in enumerate(pair_lanes)
              for hh, q in enumerate(_half_masks(q_ref[rows, sl]))]
    raw = [[_dot_nt(q, keys(sl)) for keys, _, _, _ in segments] for _, q, sl in chains]
    probs, sums = [], []
    for (h, _, _), head_raw in zip(chains, raw):
        scores = []
        for s, (_, _, bias, live) in zip(head_raw, segments):
            s = s + bias(h)
            scores.append(s if live is None else jnp.where(live, s, NEG_INF))
        m = functools.reduce(jnp.maximum, [s.max(axis=-1, keepdims=True) for s in scores])
        ps = [jnp.exp(s - m) for s in scores]
        sums.append(functools.reduce(jnp.add, [p.sum(axis=-1, keepdims=True) for p in ps]))
        probs.append([p.astype(BF16) for p in ps])
    outs = []
    for (_, _, sl), ps, l in zip(chains, probs, sums):
        acc = functools.reduce(
            jnp.add, [_dot(p, values(sl)) for p, (_, values, _, _) in zip(ps, segments)])
        outs.append(acc / l)
    for p, sl in enumerate(pair_lanes):
        o_ref[rows, sl] = _pair_select(outs[2 * p], outs[2 * p + 1]).astype(o_ref.dtype)


def _band_kernel(q_ref, k_ref, v_ref, bias_ref, o_ref, *, tq, nkb):
    def tile(i, _):
        segments = []
        for d in range(nkb):
            kbi = i - (nkb - 1) + d
            start = pl.multiple_of(jnp.maximum(kbi, 0) * tq, tq)
            segments.append((lambda sl, start=start: k_ref[pl.ds(start, tq), sl],
                             lambda sl, start=start: v_ref[pl.ds(start, tq), sl],
                             lambda h, d=d: bias_ref[h, d], kbi >= 0))
        _band_core(q_ref, o_ref, pl.ds(pl.multiple_of(i * tq, tq), tq), segments)
        return 0

    lax.fori_loop(0, q_ref.shape[0] // tq, tile, 0)


def _band_sample_kernel(q_ref, kc_ref, vc_ref, kn_ref, vn_ref, bc_ref, bn_ref, o_ref):
    _band_core(q_ref, o_ref, slice(None), (
        (lambda sl: kc_ref[:, sl], lambda sl: vc_ref[:, sl], lambda h: bc_ref[h], None),
        (lambda sl: kn_ref[:, sl], lambda sl: vn_ref[:, sl], lambda h: bn_ref[h], None)))


BAND_PAIRS_PER_STEP = 2


def _band_attn(q, k, v, bias, *, tq, nkb):
    nb, sq, w = q.shape
    pw = BAND_PAIRS_PER_STEP * 2 * HEAD_DIM
    heads_per_step = 2 * BAND_PAIRS_PER_STEP
    assert bias.shape == (H_B, nkb, tq, tq) and w % pw == 0 and sq % tq == 0
    kern = functools.partial(_band_kernel, tq=tq, nkb=nkb)
    seq_spec = pl.BlockSpec((None, sq, pw), lambda b, p: (b, 0, p))
    return pl.pallas_call(
        kern,
        grid=(nb, w // pw),
        in_specs=[seq_spec, seq_spec, seq_spec,
                  pl.BlockSpec((heads_per_step, nkb, tq, tq), lambda b, p: (p, 0, 0, 0))],
        out_specs=seq_spec,
        out_shape=jax.ShapeDtypeStruct((nb, sq, w), BF16),
        compiler_params=_params(2),
        name="band_attn",
    )(q, k, v, bias)


def _band_sample_attn(q, k_cache, v_cache, k_new, v_new, bias_cache, bias_new):
    nb, t, w = q.shape
    n_past = k_cache.shape[1]
    pw = BAND_PAIRS_PER_STEP * 2 * HEAD_DIM
    heads_per_step = 2 * BAND_PAIRS_PER_STEP
    assert bias_cache.shape == (H_B, t, n_past) and bias_new.shape == (H_B, t, t)
    spec = lambda rows: pl.BlockSpec((None, rows, pw), lambda b, p: (b, 0, p))
    bias_spec = lambda cols: pl.BlockSpec((heads_per_step, t, cols), lambda b, p: (p, 0, 0))
    return pl.pallas_call(
        _band_sample_kernel,
        grid=(nb, w // pw),
        in_specs=[spec(t), spec(n_past), spec(n_past), spec(t), spec(t),
                  bias_spec(n_past), bias_spec(t)],
        out_specs=spec(t),
        out_shape=jax.ShapeDtypeStruct((nb, t, w), BF16),
        compiler_params=_params(2),
        name="band_sample_attn",
    )(q, k_cache, v_cache, k_new, v_new, bias_cache, bias_new)


def _band_bias_kernel(tab_ref, o_ref, *, tq, tk, nkb, base_off, band_mask, kv_len):
    h = pl.program_id(0)
    d = pl.program_id(1)
    row = lax.broadcasted_iota(jnp.int32, (tq, tk), 0)
    col = lax.broadcasted_iota(jnp.int32, (tq, tk), 1)
    width = _round_up(tq + tk - 1, LANES)
    u = lax.broadcasted_iota(jnp.int32, (SUBLANES, width), 1)
    col_minus_row = jnp.where(u < tk, u, u - width)
    dist = (base_off - d * tk) - col_minus_row
    idx = jnp.clip(dist, -(CHUNK - 1), REL_MAX) + (CHUNK - 1)

    def body(r, acc):
        return jnp.where(idx == r, tab_ref[h, r], acc)

    line = lax.fori_loop(0, N_REL, body, jnp.zeros((SUBLANES, width), F32), unroll=16)
    spread = jnp.broadcast_to(line[0:1, :], (tq, width))
    bias = pltpu.roll(spread, 0, 1, stride=1, stride_axis=0)[:, :tk]
    if band_mask:
        q_chunk = ((nkb - 1) * tk + row) >> CHUNK_SHIFT
        k_chunk = (d * tk + col) >> CHUNK_SHIFT
        gap = q_chunk - k_chunk
        valid = (gap >= 0) & (gap <= PREV_CHUNKS)
    else:
        valid = (d * tk + col) < kv_len
    o_ref[...] = jnp.where(valid, bias, NEG_INF)


def _band_bias(table, *, tq, tk, nkb, base_off, band_mask, kv_len):
    kern = functools.partial(_band_bias_kernel, tq=tq, tk=tk, nkb=nkb, base_off=base_off,
                             band_mask=band_mask, kv_len=kv_len)
    return pl.pallas_call(
        kern,
        grid=(H_B, nkb),
        in_specs=[pl.BlockSpec(memory_space=pltpu.SMEM)],
        out_specs=pl.BlockSpec((None, None, tq, tk), lambda h, d: (h, d, 0, 0)),
        out_shape=jax.ShapeDtypeStruct((H_B, nkb, tq, tk), F32),
        compiler_params=_params(2),
        name="band_bias",
    )(table)


SB_DEAD_TAIL = -110.0
LOG2_E = 1.4426950408889634


def _sb_later2(tk):
    ur = lax.broadcasted_iota(jnp.int32, (tk, tk), 0)
    uc = lax.broadcasted_iota(jnp.int32, (tk, tk), 1)
    later = jnp.where(ur > uc, -1.0, 0.0).astype(BF16)
    return jnp.concatenate([later, later], axis=0)


def _sb_init(tq, n_heads):
    return tuple((jnp.zeros((tq, 1), F32), jnp.zeros((tq, 2 * HEAD_DIM), F32))
                 for _ in range(n_heads))


def _sb_blocks(qs, blocks, carry):
    zs = [[_dot_nt(q, kb) for q, kb in zip(qs, kbs)] for kbs, _, _, _ in blocks]
    sps, log_sigs = [], []
    for block_zs, (_, _, _, valid) in zip(zs, blocks):
        block_sps, block_log_sigs = [], []
        for z in block_zs:
            t = jnp.log(1.0 + jnp.exp2(jnp.abs(z) * -LOG2_E))
            sp = jnp.maximum(z, 0.0) + t
            block_log_sigs.append(z - sp)
            block_sps.append(sp if valid is None else jnp.where(valid, sp, 0.0))
        sps.append(block_sps)
        log_sigs.append(block_log_sigs)
    tails = []
    for b, (block_sps, (_, _, later2, _)) in enumerate(zip(sps, blocks)):
        block_tails = []
        for sp, (tail0, _) in zip(block_sps, carry):
            sp_hi = sp.astype(BF16)
            sp_lo = (sp - sp_hi.astype(F32)).astype(BF16)
            inner = _dot(jnp.concatenate([sp_hi, sp_lo], axis=1), later2)
            block_tails.append(inner + tail0 if b == 0 else inner)
        tails.append(block_tails)
    out = []
    for h, (tail0, acc) in enumerate(carry):
        for b, (_, vbs, _, valid) in enumerate(blocks):
            tail = tails[b][h] if b == 0 else tails[b][h] + tail0
            w = jnp.exp(log_sigs[b][h] + tail)
            if valid is not None:
                w = jnp.where(valid, w, 0.0)
            acc = acc + _dot(w.astype(BF16), vbs[h])
            tail0 = tail0 - jnp.sum(sps[b][h], axis=-1, keepdims=True)
        out.append((tail0, acc))
    return tuple(out)


def _sb_block(qs, kbs, vbs, carry, later2, valid):
    return _sb_blocks(qs, [(kbs, vbs, later2, valid)], carry)


def _sb_sweep(block, n_blocks, carry):
    def alive(c):
        return functools.reduce(jnp.maximum, [jnp.max(t0) for t0, _ in c]) > SB_DEAD_TAIL

    def cond(state):
        j, live, _ = state
        return (j >= 0) & live

    def body(state):
        j, _, c = state
        c = block(j, c)
        return j - 1, alive(c), c

    return lax.while_loop(cond, body, (n_blocks - 1, alive(carry), carry))[2]


def _sb_kernel(q_ref, k_ref, v_ref, o_ref, *, tq):
    sq = q_ref.shape[0]
    n_pairs = q_ref.shape[1] // (2 * HEAD_DIM)
    pair_lanes = [slice(p * 2 * HEAD_DIM, (p + 1) * 2 * HEAD_DIM) for p in range(n_pairs)]
    row = lax.broadcasted_iota(jnp.int32, (tq, tq), 0)
    col = lax.broadcasted_iota(jnp.int32, (tq, tq), 1)
    below_diagonal = col < row
    later2 = _sb_later2(tq)

    def tile(i, first):
        q_start = pl.multiple_of(i * tq, tq)
        chains = [(q, sl) for sl in pair_lanes
                  for q in _half_masks(q_ref[pl.ds(q_start, tq), sl])]
        qs = [q for q, _ in chains]

        def block(j, valid):
            start = pl.multiple_of(j * tq, tq)
            return ([k_ref[pl.ds(start, tq), sl] for _, sl in chains],
                    [v_ref[pl.ds(start, tq), sl] for _, sl in chains], later2, valid)

        init = _sb_init(tq, len(chains))
        if first:
            carry = _sb_blocks(qs, [block(i, below_diagonal)], init)
        else:
            carry = _sb_blocks(qs, [block(i, below_diagonal), block(i - 1, None)], init)
            carry = _sb_sweep(lambda j, c: _sb_blocks(qs, [block(j, None)], c), i - 1, carry)
        for p, sl in enumerate(pair_lanes):
            o_ref[pl.ds(q_start, tq), sl] = _pair_select(
                carry[2 * p][1], carry[2 * p + 1][1]).astype(o_ref.dtype)

    tile(0, True)

    def later_tile(i, _):
        tile(i, False)
        return 0

    lax.fori_loop(1, sq // tq, later_tile, 0)


SB_PAIRS_PER_STEP = 2


def _sb_attn(q, k, v, *, tq):
    nb, sq, w = q.shape
    sk = k.shape[1]
    pw = SB_PAIRS_PER_STEP * 2 * HEAD_DIM
    assert sk == sq and sq % tq == 0 and w % pw == 0
    kern = functools.partial(_sb_kernel, tq=tq)
    return pl.pallas_call(
        kern,
        grid=(nb, w // pw),
        in_specs=[pl.BlockSpec((None, sq, pw), lambda b, p: (b, 0, p)),
                  pl.BlockSpec((None, sk, pw), lambda b, p: (b, 0, p)),
                  pl.BlockSpec((None, sk, pw), lambda b, p: (b, 0, p))],
        out_specs=pl.BlockSpec((None, sq, pw), lambda b, p: (b, 0, p)),
        out_shape=jax.ShapeDtypeStruct((nb, sq, w), BF16),
        compiler_params=_params(2),
        name="sb_attn",
    )(q, k, v)


SB_CACHE_BLOCK = 256


def _sb_sample_kernel(q_ref, kc_ref, vc_ref, kn_ref, vn_ref, o_ref, *, tc):
    t = q_ref.shape[0]
    past = kc_ref.shape[0]
    n_pairs = q_ref.shape[1] // (2 * HEAD_DIM)
    pair_lanes = [slice(p * 2 * HEAD_DIM, (p + 1) * 2 * HEAD_DIM) for p in range(n_pairs)]
    qs = [q for sl in pair_lanes for q in _half_masks(q_ref[:, sl])]
    per_head = lambda blocks: [b for b in blocks for _ in range(2)]
    row = lax.broadcasted_iota(jnp.int32, (t, t), 0)
    col = lax.broadcasted_iota(jnp.int32, (t, t), 1)
    carry = _sb_block(qs, per_head([kn_ref[:, sl] for sl in pair_lanes]),
                      per_head([vn_ref[:, sl] for sl in pair_lanes]),
                      _sb_init(t, len(qs)), _sb_later2(t), col < row)
    later2 = _sb_later2(tc)

    def block(j, c):
        start = pl.multiple_of(j * tc, tc)
        kbs = [kc_ref[pl.ds(start, tc), sl] for sl in pair_lanes]
        vbs = [vc_ref[pl.ds(start, tc), sl] for sl in pair_lanes]
        return _sb_block(qs, per_head(kbs), per_head(vbs), c, later2, None)

    carry = _sb_sweep(block, past // tc, carry)
    for p, sl in enumerate(pair_lanes):
        o_ref[:, sl] = _pair_select(carry[2 * p][1], carry[2 * p + 1][1]).astype(o_ref.dtype)


def _sb_sample_attn(q, k_cache, v_cache, k_new, v_new):
    nb, t, w = q.shape
    past = k_cache.shape[1]
    pw = SB_PAIRS_PER_STEP * 2 * HEAD_DIM
    tc = min(SB_CACHE_BLOCK, past)
    assert past % tc == 0 and w % pw == 0
    kern = functools.partial(_sb_sample_kernel, tc=tc)
    new_spec = pl.BlockSpec((None, t, pw), lambda b, p: (b, 0, p))
    cache_spec = pl.BlockSpec((None, past, pw), lambda b, p: (b, 0, p))
    return pl.pallas_call(
        kern,
        grid=(nb, w // pw),
        in_specs=[new_spec, cache_spec, cache_spec, new_spec, new_spec],
        out_specs=new_spec,
        out_shape=jax.ShapeDtypeStruct((nb, t, w), BF16),
        compiler_params=_params(2),
        name="sb_sample_attn",
    )(q, k_cache, v_cache, k_new, v_new)


HC = H_C * HEAD_DIM


def _odd_proj_kernel(x_ref, w_ref, q_ref, k32_ref, v32_ref, k16_ref, v16_ref):
    xb = x_ref[...].astype(BF16)
    h = _dot(xb, w_ref[...])
    q_ref[...] = (h[:, 0:HC] * QK_SCALE).astype(BF16)
    k = h[:, HC:2 * HC]
    v = h[:, 2 * HC:3 * HC]
    k32_ref[...] = k
    v32_ref[...] = v
    k16_ref[...] = k.astype(BF16)
    v16_ref[...] = v.astype(BF16)


def _odd_proj(x, w):
    t, d = x.shape
    tm = min(ROW_TILE, t)
    return pl.pallas_call(
        _odd_proj_kernel,
        grid=(t // tm,),
        in_specs=[_rows(tm, d), _resident(w.shape)],
        out_specs=(_rows(tm, HC),) * 5,
        out_shape=(jax.ShapeDtypeStruct((t, HC), BF16), jax.ShapeDtypeStruct((t, HC), F32),
                   jax.ShapeDtypeStruct((t, HC), F32), jax.ShapeDtypeStruct((t, HC), BF16),
                   jax.ShapeDtypeStruct((t, HC), BF16)),
        compiler_params=_params(1, V7X_VMEM_LIMIT_BYTES),
        name="odd_in_proj",
    )(x, w)


def _post_block_kernel(*refs, n_mix, alpha):
    x_ref = refs[0]
    mix_refs = refs[1:1 + n_mix]
    w_refs = refs[1 + n_mix:1 + 2 * n_mix]
    g1_ref, b1_ref, wu_ref, wd_ref, g2_ref, b2_ref, o_ref = refs[1 + 2 * n_mix:]
    y = _dot(mix_refs[0][...], w_refs[0][...])
    for m_ref, w_ref in zip(mix_refs[1:], w_refs[1:]):
        y = y + _dot(m_ref[...], w_ref[...])
    x = _layer_norm(alpha * x_ref[...] + y, g1_ref[...], b1_ref[...])
    h = _dot(x.astype(BF16), wu_ref[...])
    h = jnp.maximum(h, 0.0)
    ff = _dot((h * h).astype(BF16), wd_ref[...])
    o_ref[...] = _layer_norm(alpha * x + ff, g2_ref[...], b2_ref[...])


def _post_block(x, mixes, ws, g1, b1, wu, wd, g2, b2, alpha):
    t, d = x.shape
    tm = min(ROW_TILE, t)
    kern = functools.partial(_post_block_kernel, n_mix=len(mixes), alpha=alpha)
    consts = list(ws) + [g1, b1, wu, wd, g2, b2]
    return pl.pallas_call(
        kern,
        grid=(t // tm,),
        in_specs=([_rows(tm, d)] + [_rows(tm, m.shape[1]) for m in mixes]
                  + [_resident(c.shape) for c in consts]),
        out_specs=_rows(tm, d),
        out_shape=jax.ShapeDtypeStruct((t, d), F32),
        compiler_params=_params(1, V7X_VMEM_LIMIT_BYTES),
        name="post_block",
    )(x, *mixes, *consts)


def _rope_tables(pos, reps):
    inv = jnp.power(ROPE_THETA, -jnp.arange(ROPE_HALF, dtype=F32) / ROPE_HALF)
    ang = pos.astype(F32)[:, None] * inv[None, :]
    cos, sin = jnp.cos(ang), jnp.sin(ang)
    n = pos.shape[0]
    z = lambda w: jnp.zeros((n, w), F32)
    tail = LANES - ROPE_LANE0 - ROPE_DIM
    c = jnp.concatenate([jnp.ones((n, ROPE_LANE0), F32), cos, cos, z(tail)], axis=1)
    s1 = jnp.concatenate([z(ROPE_LANE0), -sin, z(ROPE_HALF), z(tail)], axis=1)
    s2 = jnp.concatenate([z(ROPE_LANE0), z(ROPE_HALF), sin, z(tail)], axis=1)
    return tuple(jnp.tile(a, (reps, 1)) for a in (c, s1, s2))


CAST_ROWS = 256


def _cast_kernel(w_ref, o_ref):
    o_ref[...] = w_ref[...].astype(o_ref.dtype)


def _to_bf16(w):
    r, c = w.shape
    return pl.pallas_call(
        _cast_kernel,
        grid=(r // CAST_ROWS,),
        in_specs=[_rows(CAST_ROWS, c)],
        out_specs=_rows(CAST_ROWS, c),
        out_shape=jax.ShapeDtypeStruct((r, c), BF16),
        compiler_params=_params(1),
        name="weight_cast",
    )(w)


def _even_weights(w_in, g_q, w_uq, g_kv, w_ukv, w_out):
    d = w_in.shape[0]
    o_r = Q_LORA + KV_LORA
    tail = LANES - ROPE_LANE0 - ROPE_DIM
    w_in = _to_bf16(w_in)
    w1 = jnp.concatenate(
        [w_in[:, :o_r], jnp.zeros((d, ROPE_LANE0), BF16), w_in[:, o_r:o_r + ROPE_DIM],
         jnp.zeros((d, tail), BF16), w_in[:, o_r + ROPE_DIM:]], axis=1)
    per_head = NOPE_DIM + ROPE_DIM
    wuq = jnp.pad(_to_bf16(w_uq).reshape(Q_LORA, H_A, per_head),
                  ((0, 0), (0, 0), (0, MLA_HEAD_LANES - per_head)))
    wuq = wuq.reshape(Q_LORA, H_A * MLA_HEAD_LANES)
    wkv = _to_bf16(w_ukv).reshape(KV_LORA, H_A, NOPE_DIM + V_DIM)
    wk = jnp.pad(wkv[:, :, :NOPE_DIM], ((0, 0), (0, 0), (0, MLA_HEAD_LANES - NOPE_DIM)))
    wk = wk.reshape(KV_LORA, H_A * MLA_HEAD_LANES)
    wv = wkv[:, :, NOPE_DIM:].reshape(KV_LORA, H_A * V_DIM)
    wo = _to_bf16(w_out)
    wo_a, wo_b = wo[:H_A * V_DIM], wo[H_A * V_DIM:]
    return dict(w1=w1, gq=g_q[None, :], gkv=g_kv[None, :], wuq=wuq, wk=wk, wv=wv,
                wo_a=wo_a, wo_b=wo_b)


def _round_up(n, m):
    return (n + m - 1) // m * m


ATTN_TILE = 256


def _even_mixer(x, nb, t, ropes, w, table, cache):
    qpad, ckv, krw, qb, kb32, vb32, kb16, vb16 = _even_proj(x, w["w1"], w["gq"], w["gkv"], w["wuq"], ropes)
    r3 = lambda a: a.reshape(nb, t, a.shape[-1])
    qpad, qb, kb16, vb16 = map(r3, (qpad, qb, kb16, vb16))
    kb32 = kb32.reshape(nb, t, H_B, HEAD_DIM)
    vb32 = vb32.reshape(nb, t, H_B, HEAD_DIM)
    if cache is None:
        tq = min(ATTN_TILE, t)
        tk = min(2 * tq, t)
        kpad, vt = _kv_up_blocked(ckv, krw, w["wk"], w["wv"].T, nb, t, tk)
        o_a = _mla_prompt_attn(qpad, r3(kpad), vt, tq=tq, tk=tk)
        nkb = min(BAND_PAST // tq + 1, t // tq)
        bias = _band_bias(table, tq=tq, tk=tq, nkb=nkb, base_off=(nkb - 1) * tq,
                          band_mask=True, kv_len=t)
        o_b = _band_attn(qb, kb16, vb16, bias, tq=tq, nkb=nkb)
        rows = min(BAND_PAST, t)
        new_bk, new_bv = kb32[:, t - rows:], vb32[:, t - rows:]
    else:
        c_ckv, c_kr, c_bk, c_bv = cache
        past = c_ckv.shape[1]
        tail = LANES - ROPE_LANE0 - ROPE_DIM
        c_krw = jnp.pad(c_kr, ((0, 0), (0, 0), (ROPE_LANE0, tail)))
        kpad_c, v16_c = _kv_up(c_ckv.reshape(nb * past, KV_LORA), c_krw.reshape(nb * past, LANES),
                               w["wk"], w["wv"])
        kpad, v16 = _kv_up(ckv, krw, w["wk"], w["wv"])
        assert past % CHUNK == 0 and t <= CHUNK
        o_a = _mla_sample_attn(qpad, kpad_c.reshape(nb, past, -1), v16_c.reshape(nb, past, -1),
                               r3(kpad), r3(v16))
        n_past = c_bk.shape[1]
        band_len = n_past + t
        bias = _band_bias(table, tq=t, tk=_round_up(band_len, LANES), nkb=1, base_off=n_past,
                          band_mask=False, kv_len=band_len)[:, 0]
        flat16 = lambda c: c.reshape(nb, n_past, HB).astype(BF16)
        o_b = _band_sample_attn(qb, flat16(c_bk), flat16(c_bv), kb16, vb16,
                                bias[:, :, :n_past], bias[:, :, n_past:band_len])
        rows = min(BAND_PAST, band_len)
        new_bk = jnp.concatenate([c_bk, kb32], axis=1)[:, band_len - rows:]
        new_bv = jnp.concatenate([c_bv, vb32], axis=1)[:, band_len - rows:]
    mixes = (o_a.reshape(nb * t, -1), o_b.reshape(nb * t, -1))
    new_ckv = ckv.reshape(nb, t, KV_LORA)
    new_kr = krw[:, ROPE_LANE0:ROPE_LANE0 + ROPE_DIM].reshape(nb, t, ROPE_DIM)
    return mixes, (w["wo_a"], w["wo_b"]), (new_ckv, new_kr, new_bk, new_bv)


def _odd_mixer(x, nb, t, w_in, w_out, cache):
    q16, k32, v32, k16, v16 = _odd_proj(x, w_in)
    r3 = lambda a: a.reshape(nb, t, HC)
    q16, k16, v16 = map(r3, (q16, k16, v16))
    if cache is None:
        tq = min(ATTN_TILE, t)
        o = _sb_attn(q16, k16, v16, tq=tq)
    else:
        c_k, c_v = cache
        past = c_k.shape[1]
        flat16 = lambda c: c.reshape(nb, past, HC).astype(BF16)
        o = _sb_sample_attn(q16, flat16(c_k), flat16(c_v), k16, v16)
    new_k = k32.reshape(nb, t, H_C, HEAD_DIM)
    new_v = v32.reshape(nb, t, H_C, HEAD_DIM)
    return (o.reshape(nb * t, HC),), (w_out,), (new_k, new_v)


def kernel(x_prompt, x_sample, cache_mla_ckv, cache_mla_krope, cache_band_k, cache_band_v,
           cache_sb_k, cache_sb_v, w_in_ab, g_q_lat, w_uq, g_kv_lat, w_ukv, rel_bias, w_out_ab,
           w_in_c, w_out_c, ln_mix_g, ln_mix_b, ln_ffn_g, ln_ffn_b, w_ff_up, w_ff_down):
    nb_p, t_p, d = x_prompt.shape
    nb_s, t_s, _ = x_sample.shape
    past = cache_mla_ckv.shape[2]
    depth = ln_mix_g.shape[0]
    alpha = (2.0 * depth) ** 0.25
    xp = x_prompt.reshape(nb_p * t_p, d)
    xs = x_sample.reshape(nb_s * t_s, d)
    ropes_p = _rope_tables(jnp.arange(t_p, dtype=jnp.int32), max(1, ROW_TILE // t_p))
    ropes_s = _rope_tables(past + jnp.arange(t_s, dtype=jnp.int32), max(1, ROW_TILE // t_s))
    even_p, even_s, odd_p, odd_s = [], [], [], []
    for l in range(depth):
        i = l // 2
        if l % 2 == 0:
            w = _even_weights(w_in_ab[i], g_q_lat[i], w_uq[i], g_kv_lat[i], w_ukv[i], w_out_ab[i])
            mp, wo, new_p = _even_mixer(xp, nb_p, t_p, ropes_p, w, rel_bias[i], None)
            ms, _, new_s = _even_mixer(
                xs, nb_s, t_s, ropes_s, w, rel_bias[i],
                (cache_mla_ckv[i], cache_mla_krope[i], cache_band_k[i], cache_band_v[i]))
            even_p.append(new_p)
            even_s.append(new_s)
        else:
            w_in = _to_bf16(w_in_c[i])
            w_out = _to_bf16(w_out_c[i])
            mp, wo, new_p = _odd_mixer(xp, nb_p, t_p, w_in, w_out, None)
            ms, _, new_s = _odd_mixer(xs, nb_s, t_s, w_in, w_out, (cache_sb_k[i], cache_sb_v[i]))
            odd_p.append(new_p)
            odd_s.append(new_s)
        g1, b1 = ln_mix_g[l][None, :], ln_mix_b[l][None, :]
        g2, b2 = ln_ffn_g[l][None, :], ln_ffn_b[l][None, :]
        wu, wd = _to_bf16(w_ff_up[l]), _to_bf16(w_ff_down[l])
        xp = _post_block(xp, mp, wo, g1, b1, wu, wd, g2, b2, alpha)
        xs = _post_block(xs, ms, wo, g1, b1, wu, wd, g2, b2, alpha)
    stack = lambda groups, k: jnp.stack([g[k] for g in groups])
    return (xp.reshape(nb_p, t_p, d), xs.reshape(nb_s, t_s, d),
            stack(even_p, 0), stack(even_p, 1), stack(even_p, 2), stack(even_p, 3),
            stack(odd_p, 0), stack(odd_p, 1),
            stack(even_s, 0), stack(even_s, 1), stack(even_s, 2), stack(even_s, 3),
            stack(odd_s, 0), stack(odd_s, 1))
```

```python
import functools

import jax
import jax.numpy as jnp
from jax import lax
from jax.experimental import pallas as pl
from jax.experimental.pallas import tpu as pltpu

F32 = jnp.float32
BF16 = jnp.bfloat16

CHUNK = 64
CHUNK_SHIFT = 6
HEAD_DIM = 64
H_A = 8
Q_LORA = 768
KV_LORA = 256
NOPE_DIM = 64
ROPE_DIM = 32
V_DIM = 64
ROPE_THETA = 10000.0
MLA_SCALE = (NOPE_DIM + ROPE_DIM) ** -0.5
H_B = 8
PREV_CHUNKS = 8
BAND_PAST = PREV_CHUNKS * CHUNK
REL_MAX = 256
N_REL = REL_MAX + CHUNK
H_C = 16
QK_SCALE = HEAD_DIM ** -0.5
NEG_INF = -1e30

LANES = 128
SUBLANES = 8
V7X_VMEM_LIMIT_BYTES = 56 * 1024 * 1024

MLA_HEAD_LANES = LANES
ROPE_LANE0 = NOPE_DIM
ROPE_HALF = ROPE_DIM // 2

ROW_TILE = 512


def _params(n_axes, vmem_bytes=None):
    return pltpu.CompilerParams(
        dimension_semantics=("parallel",) * n_axes,
        vmem_limit_bytes=vmem_bytes,
    )


def _resident(shape):
    nd = len(shape)
    return pl.BlockSpec(shape, lambda *_: (0,) * nd, pipeline_mode=pl.Buffered(1))


def _rows(tm, width):
    return pl.BlockSpec((tm, width), lambda i: (i, 0))


def _dot(a, b):
    return jnp.dot(a, b, preferred_element_type=F32)


def _dot_nt(a, b):
    return lax.dot_general(a, b, (((1,), (1,)), ((), ())), preferred_element_type=F32)


def _rms(h, g, eps=1e-6):
    return h * lax.rsqrt(jnp.mean(h * h, axis=-1, keepdims=True) + eps) * g


def _layer_norm(r, g, b, eps=1e-5):
    mu = jnp.mean(r, axis=-1, keepdims=True)
    d = r - mu
    var = jnp.mean(d * d, axis=-1, keepdims=True)
    return d * lax.rsqrt(var + eps) * g + b


N_BAND = 3 * H_B * HEAD_DIM
W1_Q0, W1_Q1 = 0, Q_LORA
W1_C0, W1_C1 = W1_Q1, W1_Q1 + KV_LORA
W1_R0, W1_R1 = W1_C1, W1_C1 + LANES
W1_B0, W1_B1 = W1_R1, W1_R1 + N_BAND
HB = H_B * HEAD_DIM


def _even_proj_kernel(x_ref, w1_ref, gq_ref, gkv_ref, wuq_ref, c_ref, s1_ref, s2_ref,
                      qpad_ref, ckv_ref, krw_ref, qb_ref, kb32_ref, vb32_ref, kb16_ref, vb16_ref):
    xb = x_ref[...].astype(BF16)
    c = c_ref[...]
    s1 = s1_ref[...]
    s2 = s2_ref[...]

    def rope(v):
        return (v * c + pltpu.roll(v, LANES - ROPE_HALF, 1) * s1
                + pltpu.roll(v, ROPE_HALF, 1) * s2)

    hq = _dot(xb, w1_ref[:, W1_Q0:W1_Q1])
    qn = _rms(hq, gq_ref[...]).astype(BF16)
    qa = _dot(qn, wuq_ref[...])
    for h in range(H_A):
        sl = slice(h * MLA_HEAD_LANES, (h + 1) * MLA_HEAD_LANES)
        qpad_ref[:, sl] = rope(qa[:, sl]).astype(BF16)

    hc = _dot(xb, w1_ref[:, W1_C0:W1_C1])
    ckv_ref[...] = _rms(hc, gkv_ref[...])

    hk = _dot(xb, w1_ref[:, W1_R0:W1_R1])
    krw_ref[...] = rope(hk)

    hb = _dot(xb, w1_ref[:, W1_B0:W1_B1])
    qb_ref[...] = (hb[:, 0:HB] * QK_SCALE).astype(BF16)
    kb = hb[:, HB:2 * HB]
    vb = hb[:, 2 * HB:3 * HB]
    kb32_ref[...] = kb
    vb32_ref[...] = vb
    kb16_ref[...] = kb.astype(BF16)
    vb16_ref[...] = vb.astype(BF16)


def _even_proj(x, w1, gq, gkv, wuq, rope_tabs):
    t, d = x.shape
    tm = min(ROW_TILE, t)
    c, s1, s2 = rope_tabs
    period_tiles = c.shape[0] // tm
    assert c.shape[0] % tm == 0
    rope_spec = pl.BlockSpec((tm, LANES), lambda i: (i % period_tiles, 0))
    qw = H_A * MLA_HEAD_LANES
    out_shape = (
        jax.ShapeDtypeStruct((t, qw), BF16),
        jax.ShapeDtypeStruct((t, KV_LORA), F32),
        jax.ShapeDtypeStruct((t, LANES), F32),
        jax.ShapeDtypeStruct((t, HB), BF16),
        jax.ShapeDtypeStruct((t, HB), F32),
        jax.ShapeDtypeStruct((t, HB), F32),
        jax.ShapeDtypeStruct((t, HB), BF16),
        jax.ShapeDtypeStruct((t, HB), BF16),
    )
    return pl.pallas_call(
        _even_proj_kernel,
        grid=(t // tm,),
        in_specs=[_rows(tm, d), _resident(w1.shape), _resident(gq.shape), _resident(gkv.shape),
                  _resident(wuq.shape), rope_spec, rope_spec, rope_spec],
        out_specs=(_rows(tm, qw), _rows(tm, KV_LORA), _rows(tm, LANES), _rows(tm, HB),
                   _rows(tm, HB), _rows(tm, HB), _rows(tm, HB), _rows(tm, HB)),
        out_shape=out_shape,
        compiler_params=_params(1, V7X_VMEM_LIMIT_BYTES),
        name="even_in_proj",
    )(x, w1, gq, gkv, wuq, c, s1, s2)


def _kv_up_kernel(ckv_ref, krw_ref, wk_ref, wv_ref, kpad_ref, v_ref, *, values_on_rows):
    cb = ckv_ref[...].astype(BF16)
    krw = krw_ref[...]
    k = _dot(cb, wk_ref[...])
    for h in range(H_A):
        sl = slice(h * MLA_HEAD_LANES, (h + 1) * MLA_HEAD_LANES)
        kpad_ref[:, sl] = (k[:, sl] + krw).astype(BF16)
    if values_on_rows:
        v_ref[...] = _dot_nt(wv_ref[...], cb).astype(BF16)
    else:
        v_ref[...] = _dot(cb, wv_ref[...]).astype(BF16)


def _kv_up(ckv, krw, wk, wv):
    t = ckv.shape[0]
    tm = min(ROW_TILE, t)
    kw = H_A * MLA_HEAD_LANES
    vw = H_A * V_DIM
    return pl.pallas_call(
        functools.partial(_kv_up_kernel, values_on_rows=False),
        grid=(t // tm,),
        in_specs=[_rows(tm, KV_LORA), _rows(tm, LANES), _resident(wk.shape), _resident(wv.shape)],
        out_specs=(_rows(tm, kw), _rows(tm, vw)),
        out_shape=(jax.ShapeDtypeStruct((t, kw), BF16), jax.ShapeDtypeStruct((t, vw), BF16)),
        compiler_params=_params(1),
        name="mla_kv_up",
    )(ckv, krw, wk, wv)


def _kv_up_blocked(ckv, krw, wk, wv_t, nb, seq, tk):
    t = ckv.shape[0]
    kw = H_A * MLA_HEAD_LANES
    vw = H_A * V_DIM
    per_seq = seq // tk
    return pl.pallas_call(
        functools.partial(_kv_up_kernel, values_on_rows=True),
        grid=(t // tk,),
        in_specs=[_rows(tk, KV_LORA), _rows(tk, LANES), _resident(wk.shape), _resident(wv_t.shape)],
        out_specs=(_rows(tk, kw),
                   pl.BlockSpec((None, None, vw, tk), lambda i: (i // per_seq, i % per_seq, 0, 0))),
        out_shape=(jax.ShapeDtypeStruct((t, kw), BF16),
                   jax.ShapeDtypeStruct((nb, per_seq, vw, tk), BF16)),
        compiler_params=_params(1),
        name="mla_kv_up_blocked",
    )(ckv, krw, wk, wv_t)


def _pair_select(lo, hi):
    lane = lax.broadcasted_iota(jnp.int32, lo.shape, 1)
    return jnp.where(lane < HEAD_DIM, lo, hi)


def _mla_sample_kernel(q_ref, kc_ref, vc_ref, kn_ref, vn_ref, o_ref):
    n_heads = q_ref.shape[1] // MLA_HEAD_LANES
    hsls = [slice(h * MLA_HEAD_LANES, (h + 1) * MLA_HEAD_LANES) for h in range(n_heads)]
    vsls = [slice(h // 2 * 2 * V_DIM, (h // 2 + 1) * 2 * V_DIM) for h in range(n_heads)]
    segments = ((kc_ref, vc_ref), (kn_ref, vn_ref))
    exp_scale = MLA_SCALE * LOG2_E
    raw = [[_dot_nt(q_ref[:, hsl], k_ref[:, hsl]) for k_ref, _ in segments] for hsl in hsls]
    probs, sums = [], []
    for scores in raw:
        m = functools.reduce(jnp.maximum, [s.max(axis=-1, keepdims=True) for s in scores])
        ps = [jnp.exp2((s - m) * exp_scale) for s in scores]
        sums.append(functools.reduce(jnp.add, [p.sum(axis=-1, keepdims=True) for p in ps]))
        probs.append([p.astype(BF16) for p in ps])
    outs = []
    for vsl, ps, l in zip(vsls, probs, sums):
        acc = functools.reduce(
            jnp.add, [_dot(p, v_ref[:, vsl]) for p, (_, v_ref) in zip(ps, segments)])
        outs.append(acc / l)
    for p in range(n_heads // 2):
        o_ref[:, vsls[2 * p]] = _pair_select(outs[2 * p], outs[2 * p + 1]).astype(o_ref.dtype)


MLA_PAIRS_PER_STEP = 2


def _mla_sample_attn(q, k_cache, v_cache, k_new, v_new):
    nb, t, _ = q.shape
    past = k_cache.shape[1]
    qk_w = MLA_PAIRS_PER_STEP * 2 * MLA_HEAD_LANES
    v_w = MLA_PAIRS_PER_STEP * 2 * V_DIM
    spec = lambda rows, width: pl.BlockSpec((None, rows, width), lambda b, p: (b, 0, p))
    return pl.pallas_call(
        _mla_sample_kernel,
        grid=(nb, H_A * V_DIM // v_w),
        in_specs=[spec(t, qk_w), spec(past, qk_w), spec(past, v_w), spec(t, qk_w), spec(t, v_w)],
        out_specs=spec(t, v_w),
        out_shape=jax.ShapeDtypeStruct((nb, t, H_A * V_DIM), BF16),
        compiler_params=_params(2),
        name="mla_sample_attn",
    )(q, k_cache, v_cache, k_new, v_new)


def _mla_prompt_kernel(q_ref, k_ref, vt_ref, o_ref, *, tq, tk):
    n_heads = q_ref.shape[1] // MLA_HEAD_LANES
    tiles_per_block = tk // tq
    v_row = lax.broadcasted_iota(jnp.int32, (2 * V_DIM, tq), 0)
    hsls = [slice(h * MLA_HEAD_LANES, (h + 1) * MLA_HEAD_LANES) for h in range(n_heads)]
    vsls = [slice(h // 2 * 2 * V_DIM, (h // 2 + 1) * 2 * V_DIM) for h in range(n_heads)]
    exp_scale = MLA_SCALE * LOG2_E

    def attend(qs, carry, block, span, visible):
        start = pl.multiple_of(block * tk, tk)
        ss = [_dot_nt(k_ref[pl.ds(start, span), hsl], q) for q, hsl in zip(qs, hsls)]
        ps, stats = [], []
        for s, (m, l, _) in zip(ss, carry):
            if visible is not None:
                s = jnp.where(visible, s, NEG_INF)
            m_new = jnp.maximum(m, jnp.max(s, axis=0, keepdims=True))
            alpha = jnp.exp2((m - m_new) * exp_scale)
            p = jnp.exp2((s - m_new) * exp_scale)
            ps.append(p.astype(BF16))
            stats.append((m_new, alpha, alpha * l + jnp.sum(p, axis=0, keepdims=True)))
        return tuple(
            (m_new, l, alpha * acc + _dot(vt_ref[block, vsl, 0:span], p))
            for p, vsl, (m_new, alpha, l), (_, _, acc) in zip(ps, vsls, stats, carry))

    def tile(block, g):
        i = block * tiles_per_block + g
        q_start = pl.multiple_of(i * tq, tq)
        qs = [q_ref[pl.ds(q_start, tq), hsl] for hsl in hsls]
        init = tuple((jnp.full((1, tq), NEG_INF, F32), jnp.zeros((1, tq), F32),
                      jnp.zeros((2 * V_DIM, tq), F32)) for _ in range(n_heads))
        carry = lax.fori_loop(0, block, lambda j, c: attend(qs, c, j, tk, None), init)
        span = (g + 1) * tq
        k_chunk = (lax.broadcasted_iota(jnp.int32, (span, tq), 0) + block * tk) >> CHUNK_SHIFT
        q_chunk = (lax.broadcasted_iota(jnp.int32, (span, tq), 1) + i * tq) >> CHUNK_SHIFT
        carry = attend(qs, carry, block, span, k_chunk <= q_chunk)
        outs = [acc / l for _, l, acc in carry]
        for p in range(n_heads // 2):
            pair = jnp.where(v_row < V_DIM, outs[2 * p], outs[2 * p + 1])
            o_ref[pl.ds(q_start, tq), vsls[2 * p]] = pair.T.astype(o_ref.dtype)

    def block_tiles(block, _):
        for g in range(tiles_per_block):
            tile(block, g)
        return 0

    lax.fori_loop(0, q_ref.shape[0] // tk, block_tiles, 0)


def _mla_prompt_attn(q, k, vt, *, tq, tk):
    nb, sq, _ = q.shape
    pairs = H_A // (2 * MLA_PAIRS_PER_STEP)
    qk_w = MLA_PAIRS_PER_STEP * 2 * MLA_HEAD_LANES
    v_w = MLA_PAIRS_PER_STEP * 2 * V_DIM
    assert vt.shape == (nb, sq // tk, H_A * V_DIM, tk)
    kern = functools.partial(_mla_prompt_kernel, tq=tq, tk=tk)
    return pl.pallas_call(
        kern,
        grid=(nb, pairs),
        in_specs=[pl.BlockSpec((None, sq, qk_w), lambda b, p: (b, 0, p)),
                  pl.BlockSpec((None, sq, qk_w), lambda b, p: (b, 0, p)),
                  pl.BlockSpec((None, sq // tk, v_w, tk), lambda b, p: (b, 0, p, 0))],
        out_specs=pl.BlockSpec((None, sq, v_w), lambda b, p: (b, 0, p)),
        out_shape=jax.ShapeDtypeStruct((nb, sq, H_A * V_DIM), BF16),
        compiler_params=_params(2),
        name="mla_prompt_attn",
    )(q, k, vt)


def _half_masks(x):
    lane = lax.broadcasted_iota(jnp.int32, x.shape, 1)
    zero = jnp.zeros_like(x)
    return jnp.where(lane < HEAD_DIM, x, zero), jnp.where(lane < HEAD_DIM, zero, x)


def _band_core(q_ref, o_ref, rows, segments):
    n_pairs = q_ref.shape[1] // (2 * HEAD_DIM)
    pair_lanes = [slice(p * 2 * HEAD_DIM, (p + 1) * 2 * HEAD_DIM) for p in range(n_pairs)]
    chains = [(2 * p + hh, q, sl) for p, sl in enumerate(pair_lanes)
              for hh, q in enumerate(_half_masks(q_ref[rows, sl]))]
    raw = [[_dot_nt(q, keys(sl)) for keys, _, _, _ in segments] for _, q, sl in chains]
    probs, sums = [], []
    for (h, _, _), head_raw in zip(chains, raw):
        scores = []
        for s, (_, _, bias, live) in zip(head_raw, segments):
            s = s + bias(h)
            scores.append(s if live is None else jnp.where(live, s, NEG_INF))
        m = functools.reduce(jnp.maximum, [s.max(axis=-1, keepdims=True) for s in scores])
        ps = [jnp.exp(s - m) for s in scores]
        sums.append(functools.reduce(jnp.add, [p.sum(axis=-1, keepdims=True) for p in ps]))
        probs.append([p.astype(BF16) for p in ps])
    outs = []
    for (_, _, sl), ps, l in zip(chains, probs, sums):
        acc = functools.reduce(
            jnp.add, [_dot(p, values(sl)) for p, (_, values, _, _) in zip(ps, segments)])
        outs.append(acc / l)
    for p, sl in enumerate(pair_lanes):
        o_ref[rows, sl] = _pair_select(outs[2 * p], outs[2 * p + 1]).astype(o_ref.dtype)


def _band_kernel(q_ref, k_ref, v_ref, bias_ref, o_ref, *, tq, nkb):
    def tile(i, _):
        segments = []
        for d in range(nkb):
            kbi = i - (nkb - 1) + d
            start = pl.multiple_of(jnp.maximum(kbi, 0) * tq, tq)
            segments.append((lambda sl, start=start: k_ref[pl.ds(start, tq), sl],
                             lambda sl, start=start: v_ref[pl.ds(start, tq), sl],
                             lambda h, d=d: bias_ref[h, d], kbi >= 0))
        _band_core(q_ref, o_ref, pl.ds(pl.multiple_of(i * tq, tq), tq), segments)
        return 0

    lax.fori_loop(0, q_ref.shape[0] // tq, tile, 0)


def _band_sample_kernel(q_ref, kc_ref, vc_ref, kn_ref, vn_ref, bc_ref, bn_ref, o_ref):
    _band_core(q_ref, o_ref, slice(None), (
        (lambda sl: kc_ref[:, sl], lambda sl: vc_ref[:, sl], lambda h: bc_ref[h], None),
        (lambda sl: kn_ref[:, sl], lambda sl: vn_ref[:, sl], lambda h: bn_ref[h], None)))


BAND_PAIRS_PER_STEP = 2


def _band_attn(q, k, v, bias, *, tq, nkb):
    nb, sq, w = q.shape
    pw = BAND_PAIRS_PER_STEP * 2 * HEAD_DIM
    heads_per_step = 2 * BAND_PAIRS_PER_STEP
    assert bias.shape == (H_B, nkb, tq, tq) and w % pw == 0 and sq % tq == 0
    kern = functools.partial(_band_kernel, tq=tq, nkb=nkb)
    seq_spec = pl.BlockSpec((None, sq, pw), lambda b, p: (b, 0, p))
    return pl.pallas_call(
        kern,
        grid=(nb, w // pw),
        in_specs=[seq_spec, seq_spec, seq_spec,
                  pl.BlockSpec((heads_per_step, nkb, tq, tq), lambda b, p: (p, 0, 0, 0))],
        out_specs=seq_spec,
        out_shape=jax.ShapeDtypeStruct((nb, sq, w), BF16),
        compiler_params=_params(2),
        name="band_attn",
    )(q, k, v, bias)


def _band_sample_attn(q, k_cache, v_cache, k_new, v_new, bias_cache, bias_new):
    nb, t, w = q.shape
    n_past = k_cache.shape[1]
    pw = BAND_PAIRS_PER_STEP * 2 * HEAD_DIM
    heads_per_step = 2 * BAND_PAIRS_PER_STEP
    assert bias_cache.shape == (H_B, t, n_past) and bias_new.shape == (H_B, t, t)
    spec = lambda rows: pl.BlockSpec((None, rows, pw), lambda b, p: (b, 0, p))
    bias_spec = lambda cols: pl.BlockSpec((heads_per_step, t, cols), lambda b, p: (p, 0, 0))
    return pl.pallas_call(
        _band_sample_kernel,
        grid=(nb, w // pw),
        in_specs=[spec(t), spec(n_past), spec(n_past), spec(t), spec(t),
                  bias_spec(n_past), bias_spec(t)],
        out_specs=spec(t),
        out_shape=jax.ShapeDtypeStruct((nb, t, w), BF16),
        compiler_params=_params(2),
        name="band_sample_attn",
    )(q, k_cache, v_cache, k_new, v_new, bias_cache, bias_new)


def _band_bias_kernel(tab_ref, o_ref, *, tq, tk, nkb, base_off, band_mask, kv_len):
    h = pl.program_id(0)
    d = pl.program_id(1)
    row = lax.broadcasted_iota(jnp.int32, (tq, tk), 0)
    col = lax.broadcasted_iota(jnp.int32, (tq, tk), 1)
    width = _round_up(tq + tk - 1, LANES)
    u = lax.broadcasted_iota(jnp.int32, (SUBLANES, width), 1)
    col_minus_row = jnp.where(u < tk, u, u - width)
    dist = (base_off - d * tk) - col_minus_row
    idx = jnp.clip(dist, -(CHUNK - 1), REL_MAX) + (CHUNK - 1)

    def body(r, acc):
        return jnp.where(idx == r, tab_ref[h, r], acc)

    line = lax.fori_loop(0, N_REL, body, jnp.zeros((SUBLANES, width), F32), unroll=16)
    spread = jnp.broadcast_to(line[0:1, :], (tq, width))
    bias = pltpu.roll(spread, 0, 1, stride=1, stride_axis=0)[:, :tk]
    if band_mask:
        q_chunk = ((nkb - 1) * tk + row) >> CHUNK_SHIFT
        k_chunk = (d * tk + col) >> CHUNK_SHIFT
        gap = q_chunk - k_chunk
        valid = (gap >= 0) & (gap <= PREV_CHUNKS)
    else:
        valid = (d * tk + col) < kv_len
    o_ref[...] = jnp.where(valid, bias, NEG_INF)


def _band_bias(table, *, tq, tk, nkb, base_off, band_mask, kv_len):
    kern = functools.partial(_band_bias_kernel, tq=tq, tk=tk, nkb=nkb, base_off=base_off,
                             band_mask=band_mask, kv_len=kv_len)
    return pl.pallas_call(
        kern,
        grid=(H_B, nkb),
        in_specs=[pl.BlockSpec(memory_space=pltpu.SMEM)],
        out_specs=pl.BlockSpec((None, None, tq, tk), lambda h, d: (h, d, 0, 0)),
        out_shape=jax.ShapeDtypeStruct((H_B, nkb, tq, tk), F32),
        compiler_params=_params(2),
        name="band_bias",
    )(table)


SB_DEAD_TAIL = -110.0
LOG2_E = 1.4426950408889634


def _sb_later2(tk):
    ur = lax.broadcasted_iota(jnp.int32, (tk, tk), 0)
    uc = lax.broadcasted_iota(jnp.int32, (tk, tk), 1)
    later = jnp.where(ur > uc, -1.0, 0.0).astype(BF16)
    return jnp.concatenate([later, later], axis=0)


def _sb_init(tq, n_heads):
    return tuple((jnp.zeros((tq, 1), F32), jnp.zeros((tq, 2 * HEAD_DIM), F32))
                 for _ in range(n_heads))


def _sb_blocks(qs, blocks, carry):
    zs = [[_dot_nt(q, kb) for q, kb in zip(qs, kbs)] for kbs, _, _, _ in blocks]
    sps, log_sigs = [], []
    for block_zs, (_, _, _, valid) in zip(zs, blocks):
        block_sps, block_log_sigs = [], []
        for z in block_zs:
            t = jnp.log(1.0 + jnp.exp2(jnp.abs(z) * -LOG2_E))
            sp = jnp.maximum(z, 0.0) + t
            block_log_sigs.append(z - sp)
            block_sps.append(sp if valid is None else jnp.where(valid, sp, 0.0))
        sps.append(block_sps)
        log_sigs.append(block_log_sigs)
    tails = []
    for b, (block_sps, (_, _, later2, _)) in enumerate(zip(sps, blocks)):
        block_tails = []
        for sp, (tail0, _) in zip(block_sps, carry):
            sp_hi = sp.astype(BF16)
            sp_lo = (sp - sp_hi.astype(F32)).astype(BF16)
            inner = _dot(jnp.concatenate([sp_hi, sp_lo], axis=1), later2)
            block_tails.append(inner + tail0 if b == 0 else inner)
        tails.append(block_tails)
    out = []
    for h, (tail0, acc) in enumerate(carry):
        for b, (_, vbs, _, valid) in enumerate(blocks):
            tail = tails[b][h] if b == 0 else tails[b][h] + tail0
            w = jnp.exp(log_sigs[b][h] + tail)
            if valid is not None:
                w = jnp.where(valid, w, 0.0)
            acc = acc + _dot(w.astype(BF16), vbs[h])
            tail0 = tail0 - jnp.sum(sps[b][h], axis=-1, keepdims=True)
        out.append((tail0, acc))
    return tuple(out)


def _sb_block(qs, kbs, vbs, carry, later2, valid):
    return _sb_blocks(qs, [(kbs, vbs, later2, valid)], carry)


def _sb_sweep(block, n_blocks, carry):
    def alive(c):
        return functools.reduce(jnp.maximum, [jnp.max(t0) for t0, _ in c]) > SB_DEAD_TAIL

    def cond(state):
        j, live, _ = state
        return (j >= 0) & live

    def body(state):
        j, _, c = state
        c = block(j, c)
        return j - 1, alive(c), c

    return lax.while_loop(cond, body, (n_blocks - 1, alive(carry), carry))[2]


def _sb_kernel(q_ref, k_ref, v_ref, o_ref, *, tq):
    sq = q_ref.shape[0]
    n_pairs = q_ref.shape[1] // (2 * HEAD_DIM)
    pair_lanes = [slice(p * 2 * HEAD_DIM, (p + 1) * 2 * HEAD_DIM) for p in range(n_pairs)]
    row = lax.broadcasted_iota(jnp.int32, (tq, tq), 0)
    col = lax.broadcasted_iota(jnp.int32, (tq, tq), 1)
    below_diagonal = col < row
    later2 = _sb_later2(tq)

    def tile(i, first):
        q_start = pl.multiple_of(i * tq, tq)
        chains = [(q, sl) for sl in pair_lanes
                  for q in _half_masks(q_ref[pl.ds(q_start, tq), sl])]
        qs = [q for q, _ in chains]

        def block(j, valid):
            start = pl.multiple_of(j * tq, tq)
            return ([k_ref[pl.ds(start, tq), sl] for _, sl in chains],
                    [v_ref[pl.ds(start, tq), sl] for _, sl in chains], later2, valid)

        init = _sb_init(tq, len(chains))
        if first:
            carry = _sb_blocks(qs, [block(i, below_diagonal)], init)
        else:
            carry = _sb_blocks(qs, [block(i, below_diagonal), block(i - 1, None)], init)
            carry = _sb_sweep(lambda j, c: _sb_blocks(qs, [block(j, None)], c), i - 1, carry)
        for p, sl in enumerate(pair_lanes):
            o_ref[pl.ds(q_start, tq), sl] = _pair_select(
                carry[2 * p][1], carry[2 * p + 1][1]).astype(o_ref.dtype)

    tile(0, True)

    def later_tile(i, _):
        tile(i, False)
        return 0

    lax.fori_loop(1, sq // tq, later_tile, 0)


SB_PAIRS_PER_STEP = 2


def _sb_attn(q, k, v, *, tq):
    nb, sq, w = q.shape
    sk = k.shape[1]
    pw = SB_PAIRS_PER_STEP * 2 * HEAD_DIM
    assert sk == sq and sq % tq == 0 and w % pw == 0
    kern = functools.partial(_sb_kernel, tq=tq)
    return pl.pallas_call(
        kern,
        grid=(nb, w // pw),
        in_specs=[pl.BlockSpec((None, sq, pw), lambda b, p: (b, 0, p)),
                  pl.BlockSpec((None, sk, pw), lambda b, p: (b, 0, p)),
                  pl.BlockSpec((None, sk, pw), lambda b, p: (b, 0, p))],
        out_specs=pl.BlockSpec((None, sq, pw), lambda b, p: (b, 0, p)),
        out_shape=jax.ShapeDtypeStruct((nb, sq, w), BF16),
        compiler_params=_params(2),
        name="sb_attn",
    )(q, k, v)


SB_CACHE_BLOCK = 256


def _sb_sample_kernel(q_ref, kc_ref, vc_ref, kn_ref, vn_ref, o_ref, *, tc):
    t = q_ref.shape[0]
    past = kc_ref.shape[0]
    n_pairs = q_ref.shape[1] // (2 * HEAD_DIM)
    pair_lanes = [slice(p * 2 * HEAD_DIM, (p + 1) * 2 * HEAD_DIM) for p in range(n_pairs)]
    qs = [q for sl in pair_lanes for q in _half_masks(q_ref[:, sl])]
    per_head = lambda blocks: [b for b in blocks for _ in range(2)]
    row = lax.broadcasted_iota(jnp.int32, (t, t), 0)
    col = lax.broadcasted_iota(jnp.int32, (t, t), 1)
    carry = _sb_block(qs, per_head([kn_ref[:, sl] for sl in pair_lanes]),
                      per_head([vn_ref[:, sl] for sl in pair_lanes]),
                      _sb_init(t, len(qs)), _sb_later2(t), col < row)
    later2 = _sb_later2(tc)

    def block(j, c):
        start = pl.multiple_of(j * tc, tc)
        kbs = [kc_ref[pl.ds(start, tc), sl] for sl in pair_lanes]
        vbs = [vc_ref[pl.ds(start, tc), sl] for sl in pair_lanes]
        return _sb_block(qs, per_head(kbs), per_head(vbs), c, later2, None)

    carry = _sb_sweep(block, past // tc, carry)
    for p, sl in enumerate(pair_lanes):
        o_ref[:, sl] = _pair_select(carry[2 * p][1], carry[2 * p + 1][1]).astype(o_ref.dtype)


def _sb_sample_attn(q, k_cache, v_cache, k_new, v_new):
    nb, t, w = q.shape
    past = k_cache.shape[1]
    pw = SB_PAIRS_PER_STEP * 2 * HEAD_DIM
    tc = min(SB_CACHE_BLOCK, past)
    assert past % tc == 0 and w % pw == 0
    kern = functools.partial(_sb_sample_kernel, tc=tc)
    new_spec = pl.BlockSpec((None, t, pw), lambda b, p: (b, 0, p))
    cache_spec = pl.BlockSpec((None, past, pw), lambda b, p: (b, 0, p))
    return pl.pallas_call(
        kern,
        grid=(nb, w // pw),
        in_specs=[new_spec, cache_spec, cache_spec, new_spec, new_spec],
        out_specs=new_spec,
        out_shape=jax.ShapeDtypeStruct((nb, t, w), BF16),
        compiler_params=_params(2),
        name="sb_sample_attn",
    )(q, k_cache, v_cache, k_new, v_new)


HC = H_C * HEAD_DIM


def _odd_proj_kernel(x_ref, w_ref, q_ref, k32_ref, v32_ref, k16_ref, v16_ref):
    xb = x_ref[...].astype(BF16)
    h = _dot(xb, w_ref[...])
    q_ref[...] = (h[:, 0:HC] * QK_SCALE).astype(BF16)
    k = h[:, HC:2 * HC]
    v = h[:, 2 * HC:3 * HC]
    k32_ref[...] = k
    v32_ref[...] = v
    k16_ref[...] = k.astype(BF16)
    v16_ref[...] = v.astype(BF16)


def _odd_proj(x, w):
    t, d = x.shape
    tm = min(ROW_TILE, t)
    return pl.pallas_call(
        _odd_proj_kernel,
        grid=(t // tm,),
        in_specs=[_rows(tm, d), _resident(w.shape)],
        out_specs=(_rows(tm, HC),) * 5,
        out_shape=(jax.ShapeDtypeStruct((t, HC), BF16), jax.ShapeDtypeStruct((t, HC), F32),
                   jax.ShapeDtypeStruct((t, HC), F32), jax.ShapeDtypeStruct((t, HC), BF16),
                   jax.ShapeDtypeStruct((t, HC), BF16)),
        compiler_params=_params(1, V7X_VMEM_LIMIT_BYTES),
        name="odd_in_proj",
    )(x, w)


def _post_block_kernel(*refs, n_mix, alpha):
    x_ref = refs[0]
    mix_refs = refs[1:1 + n_mix]
    w_refs = refs[1 + n_mix:1 + 2 * n_mix]
    g1_ref, b1_ref, wu_ref, wd_ref, g2_ref, b2_ref, o_ref = refs[1 + 2 * n_mix:]
    y = _dot(mix_refs[0][...], w_refs[0][...])
    for m_ref, w_ref in zip(mix_refs[1:], w_refs[1:]):
        y = y + _dot(m_ref[...], w_ref[...])
    x = _layer_norm(alpha * x_ref[...] + y, g1_ref[...], b1_ref[...])
    h = _dot(x.astype(BF16), wu_ref[...])
    h = jnp.maximum(h, 0.0)
    ff = _dot((h * h).astype(BF16), wd_ref[...])
    o_ref[...] = _layer_norm(alpha * x + ff, g2_ref[...], b2_ref[...])


def _post_block(x, mixes, ws, g1, b1, wu, wd, g2, b2, alpha):
    t, d = x.shape
    tm = min(ROW_TILE, t)
    kern = functools.partial(_post_block_kernel, n_mix=len(mixes), alpha=alpha)
    consts = list(ws) + [g1, b1, wu, wd, g2, b2]
    return pl.pallas_call(
        kern,
        grid=(t // tm,),
        in_specs=([_rows(tm, d)] + [_rows(tm, m.shape[1]) for m in mixes]
                  + [_resident(c.shape) for c in consts]),
        out_specs=_rows(tm, d),
        out_shape=jax.ShapeDtypeStruct((t, d), F32),
        compiler_params=_params(1, V7X_VMEM_LIMIT_BYTES),
        name="post_block",
    )(x, *mixes, *consts)


def _rope_tables(pos, reps):
    inv = jnp.power(ROPE_THETA, -jnp.arange(ROPE_HALF, dtype=F32) / ROPE_HALF)
    ang = pos.astype(F32)[:, None] * inv[None, :]
    cos, sin = jnp.cos(ang), jnp.sin(ang)
    n = pos.shape[0]
    z = lambda w: jnp.zeros((n, w), F32)
    tail = LANES - ROPE_LANE0 - ROPE_DIM
    c = jnp.concatenate([jnp.ones((n, ROPE_LANE0), F32), cos, cos, z(tail)], axis=1)
    s1 = jnp.concatenate([z(ROPE_LANE0), -sin, z(ROPE_HALF), z(tail)], axis=1)
    s2 = jnp.concatenate([z(ROPE_LANE0), z(ROPE_HALF), sin, z(tail)], axis=1)
    return tuple(jnp.tile(a, (reps, 1)) for a in (c, s1, s2))


CAST_ROWS = 256


def _cast_kernel(w_ref, o_ref):
    o_ref[...] = w_ref[...].astype(o_ref.dtype)


def _to_bf16(w):
    r, c = w.shape
    return pl.pallas_call(
        _cast_kernel,
        grid=(r // CAST_ROWS,),
        in_specs=[_rows(CAST_ROWS, c)],
        out_specs=_rows(CAST_ROWS, c),
        out_shape=jax.ShapeDtypeStruct((r, c), BF16),
        compiler_params=_params(1),
        name="weight_cast",
    )(w)


def _even_weights(w_in, g_q, w_uq, g_kv, w_ukv, w_out):
    d = w_in.shape[0]
    o_r = Q_LORA + KV_LORA
    tail = LANES - ROPE_LANE0 - ROPE_DIM
    w_in = _to_bf16(w_in)
    w1 = jnp.concatenate(
        [w_in[:, :o_r], jnp.zeros((d, ROPE_LANE0), BF16), w_in[:, o_r:o_r + ROPE_DIM],
         jnp.zeros((d, tail), BF16), w_in[:, o_r + ROPE_DIM:]], axis=1)
    per_head = NOPE_DIM + ROPE_DIM
    wuq = jnp.pad(_to_bf16(w_uq).reshape(Q_LORA, H_A, per_head),
                  ((0, 0), (0, 0), (0, MLA_HEAD_LANES - per_head)))
    wuq = wuq.reshape(Q_LORA, H_A * MLA_HEAD_LANES)
    wkv = _to_bf16(w_ukv).reshape(KV_LORA, H_A, NOPE_DIM + V_DIM)
    wk = jnp.pad(wkv[:, :, :NOPE_DIM], ((0, 0), (0, 0), (0, MLA_HEAD_LANES - NOPE_DIM)))
    wk = wk.reshape(KV_LORA, H_A * MLA_HEAD_LANES)
    wv = wkv[:, :, NOPE_DIM:].reshape(KV_LORA, H_A * V_DIM)
    wo = _to_bf16(w_out)
    wo_a, wo_b = wo[:H_A * V_DIM], wo[H_A * V_DIM:]
    return dict(w1=w1, gq=g_q[None, :], gkv=g_kv[None, :], wuq=wuq, wk=wk, wv=wv,
                wo_a=wo_a, wo_b=wo_b)


def _round_up(n, m):
    return (n + m - 1) // m * m


ATTN_TILE = 256


def _even_mixer(x, nb, t, ropes, w, table, cache):
    qpad, ckv, krw, qb, kb32, vb32, kb16, vb16 = _even_proj(x, w["w1"], w["gq"], w["gkv"], w["wuq"], ropes)
    r3 = lambda a: a.reshape(nb, t, a.shape[-1])
    qpad, qb, kb16, vb16 = map(r3, (qpad, qb, kb16, vb16))
    kb32 = kb32.reshape(nb, t, H_B, HEAD_DIM)
    vb32 = vb32.reshape(nb, t, H_B, HEAD_DIM)
    if cache is None:
        tq = min(ATTN_TILE, t)
        tk = min(4 * tq, t)
        kpad, vt = _kv_up_blocked(ckv, krw, w["wk"], w["wv"].T, nb, t, tk)
        o_a = _mla_prompt_attn(qpad, r3(kpad), vt, tq=tq, tk=tk)
        nkb = min(BAND_PAST // tq + 1, t // tq)
        bias = _band_bias(table, tq=tq, tk=tq, nkb=nkb, base_off=(nkb - 1) * tq,
                          band_mask=True, kv_len=t)
        o_b = _band_attn(qb, kb16, vb16, bias, tq=tq, nkb=nkb)
        rows = min(BAND_PAST, t)
        new_bk, new_bv = kb32[:, t - rows:], vb32[:, t - rows:]
    else:
        c_ckv, c_kr, c_bk, c_bv = cache
        past = c_ckv.shape[1]
        tail = LANES - ROPE_LANE0 - ROPE_DIM
        c_krw = jnp.pad(c_kr, ((0, 0), (0, 0), (ROPE_LANE0, tail)))
        kpad_c, v16_c = _kv_up(c_ckv.reshape(nb * past, KV_LORA), c_krw.reshape(nb * past, LANES),
                               w["wk"], w["wv"])
        kpad, v16 = _kv_up(ckv, krw, w["wk"], w["wv"])
        assert past % CHUNK == 0 and t <= CHUNK
        o_a = _mla_sample_attn(qpad, kpad_c.reshape(nb, past, -1), v16_c.reshape(nb, past, -1),
                               r3(kpad), r3(v16))
        n_past = c_bk.shape[1]
        band_len = n_past + t
        bias = _band_bias(table, tq=t, tk=_round_up(band_len, LANES), nkb=1, base_off=n_past,
                          band_mask=False, kv_len=band_len)[:, 0]
        flat16 = lambda c: c.reshape(nb, n_past, HB).astype(BF16)
        o_b = _band_sample_attn(qb, flat16(c_bk), flat16(c_bv), kb16, vb16,
                                bias[:, :, :n_past], bias[:, :, n_past:band_len])
        rows = min(BAND_PAST, band_len)
        new_bk = jnp.concatenate([c_bk, kb32], axis=1)[:, band_len - rows:]
        new_bv = jnp.concatenate([c_bv, vb32], axis=1)[:, band_len - rows:]
    mixes = (o_a.reshape(nb * t, -1), o_b.reshape(nb * t, -1))
    new_ckv = ckv.reshape(nb, t, KV_LORA)
    new_kr = krw[:, ROPE_LANE0:ROPE_LANE0 + ROPE_DIM].reshape(nb, t, ROPE_DIM)
    return mixes, (w["wo_a"], w["wo_b"]), (new_ckv, new_kr, new_bk, new_bv)


def _odd_mixer(x, nb, t, w_in, w_out, cache):
    q16, k32, v32, k16, v16 = _odd_proj(x, w_in)
    r3 = lambda a: a.reshape(nb, t, HC)
    q16, k16, v16 = map(r3, (q16, k16, v16))
    if cache is None:
        tq = min(ATTN_TILE, t)
        o = _sb_attn(q16, k16, v16, tq=tq)
    else:
        c_k, c_v = cache
        past = c_k.shape[1]
        flat16 = lambda c: c.reshape(nb, past, HC).astype(BF16)
        o = _sb_sample_attn(q16, flat16(c_k), flat16(c_v), k16, v16)
    new_k = k32.reshape(nb, t, H_C, HEAD_DIM)
    new_v = v32.reshape(nb, t, H_C, HEAD_DIM)
    return (o.reshape(nb * t, HC),), (w_out,), (new_k, new_v)


def kernel(x_prompt, x_sample, cache_mla_ckv, cache_mla_krope, cache_band_k, cache_band_v,
           cache_sb_k, cache_sb_v, w_in_ab, g_q_lat, w_uq, g_kv_lat, w_ukv, rel_bias, w_out_ab,
           w_in_c, w_out_c, ln_mix_g, ln_mix_b, ln_ffn_g, ln_ffn_b, w_ff_up, w_ff_down):
    nb_p, t_p, d = x_prompt.shape
    nb_s, t_s, _ = x_sample.shape
    past = cache_mla_ckv.shape[2]
    depth = ln_mix_g.shape[0]
    alpha = (2.0 * depth) ** 0.25
    xp = x_prompt.reshape(nb_p * t_p, d)
    xs = x_sample.reshape(nb_s * t_s, d)
    ropes_p = _rope_tables(jnp.arange(t_p, dtype=jnp.int32), max(1, ROW_TILE // t_p))
    ropes_s = _rope_tables(past + jnp.arange(t_s, dtype=jnp.int32), max(1, ROW_TILE // t_s))
    even_p, even_s, odd_p, odd_s = [], [], [], []
    for l in range(depth):
        i = l // 2
        if l % 2 == 0:
            w = _even_weights(w_in_ab[i], g_q_lat[i], w_uq[i], g_kv_lat[i], w_ukv[i], w_out_ab[i])
            mp, wo, new_p = _even_mixer(xp, nb_p, t_p, ropes_p, w, rel_bias[i], None)
            ms, _, new_s = _even_mixer(
                xs, nb_s, t_s, ropes_s, w, rel_bias[i],
                (cache_mla_ckv[i], cache_mla_krope[i], cache_band_k[i], cache_band_v[i]))
            even_p.append(new_p)
            even_s.append(new_s)
        else:
            w_in = _to_bf16(w_in_c[i])
            w_out = _to_bf16(w_out_c[i])
            mp, wo, new_p = _odd_mixer(xp, nb_p, t_p, w_in, w_out, None)
            ms, _, new_s = _odd_mixer(xs, nb_s, t_s, w_in, w_out, (cache_sb_k[i], cache_sb_v[i]))
            odd_p.append(new_p)
            odd_s.append(new_s)
        g1, b1 = ln_mix_g[l][None, :], ln_mix_b[l][None, :]
        g2, b2 = ln_ffn_g[l][None, :], ln_ffn_b[l][None, :]
        wu, wd = _to_bf16(w_ff_up[l]), _to_bf16(w_ff_down[l])
        xp = _post_block(xp, mp, wo, g1, b1, wu, wd, g2, b2, alpha)
        xs = _post_block(xs, ms, wo, g1, b1, wu, wd, g2, b2, alpha)
    stack = lambda groups, k: jnp.stack([g[k] for g in groups])
    return (xp.reshape(nb_p, t_p, d), xs.reshape(nb_s, t_s, d),
            stack(even_p, 0), stack(even_p, 1), stack(even_p, 2), stack(even_p, 3),
            stack(odd_p, 0), stack(odd_p, 1),
            stack(even_s, 0), stack(even_s, 1), stack(even_s, 2), stack(even_s, 3),
            stack(odd_s, 0), stack(odd_s, 1))
```
